```python
import jax, jax.numpy as jnp
from jax import lax
import numpy as np

D_MODEL = 4096
BATCH = 4
SEQ = 2048
DEPTH = 4
DEC_BATCH = 128
DEC_SEQ = 8
PAST_LEN = 16384
PAGE_SIZE = 128

POOL_WINDOWS = (2, 4, 8, 16)
POOL_GROUPS = len(POOL_WINDOWS)
POOL_WIDTH = D_MODEL // 2
POOL_GROUP_DIM = POOL_WIDTH // POOL_GROUPS
POOL_BUF = max(POOL_WINDOWS) - 1
SGU_HEADS = 8
SGU_WIDTH = D_MODEL // 2
SGU_HEAD_DIM = SGU_WIDTH // SGU_HEADS
CHUNK = 128
EPS = 1e-6
COL_SIZES = (POOL_WIDTH, POOL_WIDTH, SGU_WIDTH, SGU_WIDTH, SGU_WIDTH, D_MODEL, D_MODEL)
COL_SPLITS = tuple(int(s) for s in np.cumsum(COL_SIZES)[:-1])
IN_COLS = int(sum(COL_SIZES))

kernel_name = "hybrid_pool_sgu_decoder_step"


def rmsnorm(x, g):
    xf = x.astype(jnp.float32)
    r = xf * lax.rsqrt(jnp.mean(xf * xf, axis=-1, keepdims=True) + EPS)
    return (r * g.astype(jnp.float32)).astype(x.dtype)


def pool_mixer(a, buf, pos0, w_group, scale):
    B, T, P = a.shape
    full = jnp.concatenate([buf.astype(a.dtype), a], axis=1)
    c = jnp.cumsum(full.astype(jnp.float32), axis=1)
    c0 = jnp.concatenate([jnp.zeros((B, 1, P), jnp.float32), c], axis=1)
    pos = pos0 + jnp.arange(T, dtype=jnp.int32)
    outs = []
    for gi, w in enumerate(POOL_WINDOWS):
        sl = slice(gi * POOL_GROUP_DIM, (gi + 1) * POOL_GROUP_DIM)
        hi = c0[:, POOL_BUF + 1:POOL_BUF + 1 + T, sl]
        lo = c0[:, POOL_BUF + 1 - w:POOL_BUF + 1 - w + T, sl]
        cnt = jnp.minimum(pos + 1, w).astype(jnp.float32)[None, :, None]
        outs.append((hi - lo) / cnt)
    pooled = jnp.concatenate(outs, axis=-1) - a.astype(jnp.float32)
    pooled = pooled.reshape(B, T, POOL_GROUPS, POOL_GROUP_DIM).astype(a.dtype)
    mixed = jnp.einsum('btgc,gcd->btgd', pooled, w_group).reshape(B, T, P)
    new_buf = full[:, -POOL_BUF:]
    return mixed * scale, new_buf


def sgu_mixer(u, v, pos0, norm_g, w_s, b_s):
    assert pos0 % CHUNK == 0
    B, T, W = v.shape
    v_n = rmsnorm(v, norm_g)
    n_chunks = -(-T // CHUNK)
    Tp = n_chunks * CHUNK
    vp = jnp.pad(v_n, ((0, 0), (0, Tp - T), (0, 0)))
    vc = vp.reshape(B, n_chunks, CHUNK, SGU_HEADS, SGU_HEAD_DIM)
    mask = jnp.tril(jnp.ones((CHUNK, CHUNK), dtype=bool))
    ws = jnp.where(mask[None], w_s, jnp.zeros((), w_s.dtype))
    z = jnp.einsum('hts,bnshd->bnthd', ws, vc) + b_s.T[None, None, :, :, None]
    z = z.reshape(B, Tp, W)[:, :T]
    return u * z, v_n


def layer(x, pos0, pool_buf, norm_g, w_in, pool_w, pool_scale, sgu_norm_g, sgu_w, sgu_b,
          w_branch_a, w_branch_b, w_out):
    h = rmsnorm(x, norm_g)
    proj = jnp.einsum('btd,dc->btc', h, w_in)
    a, gate_a, u, v, gate_b, sel_a, sel_b = jnp.split(proj, COL_SPLITS, axis=-1)
    ya, new_buf = pool_mixer(a, pool_buf, pos0, pool_w, pool_scale)
    ya = ya * jax.nn.silu(gate_a)
    yb, v_n = sgu_mixer(u, v, pos0, sgu_norm_g, sgu_w, sgu_b)
    yb = yb * jax.nn.silu(gate_b)
    merged = (jax.nn.sigmoid(sel_a) * jnp.einsum('btp,pd->btd', ya, w_branch_a)
              + jax.nn.sigmoid(sel_b) * jnp.einsum('btw,wd->btd', yb, w_branch_b))
    return x + jnp.einsum('btd,de->bte', merged, w_out), new_buf, v_n


def setup_inputs(seed: int = 0) -> dict:
    key = jax.random.key(seed)
    ks = jax.random.split(key, 16)
    f32 = jnp.float32
    nrm = lambda k, shape, s: jax.random.normal(k, shape, f32) * s
    return {
        "x_prompt": nrm(ks[0], (BATCH, SEQ, D_MODEL), 1.0),
        "x_sample": nrm(ks[1], (DEC_BATCH, DEC_SEQ, D_MODEL), 1.0),
        "state_pool": nrm(ks[2], (DEPTH, DEC_BATCH, POOL_BUF, POOL_WIDTH), 1.0),
        "norm_g": 1.0 + nrm(ks[3], (DEPTH, D_MODEL), 0.02),
        "w_in": nrm(ks[4], (DEPTH, D_MODEL, IN_COLS), D_MODEL ** -0.5),
        "pool_w": nrm(ks[5], (DEPTH, POOL_GROUPS, POOL_GROUP_DIM, POOL_GROUP_DIM), POOL_GROUP_DIM ** -0.5),
        "pool_scale": 1.0 + nrm(ks[6], (DEPTH, POOL_WIDTH), 0.1),
        "sgu_norm_g": 1.0 + nrm(ks[7], (DEPTH, SGU_WIDTH), 0.02),
        "sgu_w": nrm(ks[8], (DEPTH, SGU_HEADS, CHUNK, CHUNK), 0.5 * CHUNK ** -0.5),
        "sgu_b": 1.0 + nrm(ks[9], (DEPTH, SGU_HEADS, CHUNK), 0.02),
        "w_branch_a": nrm(ks[10], (DEPTH, POOL_WIDTH, D_MODEL), POOL_WIDTH ** -0.5),
        "w_branch_b": nrm(ks[11], (DEPTH, SGU_WIDTH, D_MODEL), SGU_WIDTH ** -0.5),
        "w_out": nrm(ks[12], (DEPTH, D_MODEL, D_MODEL), D_MODEL ** -0.5),
        "final_norm_g": 1.0 + nrm(ks[13], (D_MODEL,), 0.02),
    }


def reference(x_prompt, x_sample, state_pool, norm_g, w_in, pool_w, pool_scale, sgu_norm_g,
              sgu_w, sgu_b, w_branch_a, w_branch_b, w_out, final_norm_g):
    xp = x_prompt
    xs = x_sample
    zero_buf = jnp.zeros((xp.shape[0], POOL_BUF, POOL_WIDTH), xp.dtype)
    pool_p, pool_s, sgu_v_s = [], [], []
    for l in range(DEPTH):
        params = (norm_g[l], w_in[l], pool_w[l], pool_scale[l], sgu_norm_g[l], sgu_w[l], sgu_b[l],
                  w_branch_a[l], w_branch_b[l], w_out[l])
        xp, nb_p, _ = layer(xp, 0, zero_buf, *params)
        xs, nb_s, vn_s = layer(xs, PAST_LEN, state_pool[l], *params)
        pool_p.append(nb_p)
        pool_s.append(nb_s)
        sgu_v_s.append(vn_s)
    y_prompt = rmsnorm(xp, final_norm_g)
    y_sample = rmsnorm(xs, final_norm_g)
    state_pool_prompt = jnp.stack(pool_p, axis=0)
    state_pool_sample = jnp.stack(pool_s, axis=0)
    state_sgu_v_sample = jnp.stack(sgu_v_s, axis=0)
    return (y_prompt, y_sample, state_pool_prompt, state_pool_sample, state_sgu_v_sample)
```

```python
import functools

import jax
import jax.numpy as jnp
from jax import lax
from jax.experimental import pallas as pl
from jax.experimental.pallas import tpu as pltpu

F32 = jnp.float32
BF16 = jnp.bfloat16

POOL_WINDOWS = (2, 4, 8, 16)
POOL_GROUPS = len(POOL_WINDOWS)
POOL_BUF = max(POOL_WINDOWS) - 1
HALO = POOL_BUF + 1
SGU_HEADS = 8
CHUNK = 128
EPS = 1e-6
PAST_LEN = 16384

V7X_VMEM_LIMIT_BYTES = 56 * 1024 * 1024

ROW_TILE = 1024
COL_TILE = 512
NORM_ROWS = 512
MIX_ROWS = 256
MIX_SEQS = 32


def _params(*semantics):
    return pltpu.CompilerParams(dimension_semantics=semantics,
                                vmem_limit_bytes=V7X_VMEM_LIMIT_BYTES)


def _rms(x, g):
    ms = jnp.mean(x * x, axis=-1, keepdims=True)
    return x * lax.rsqrt(ms + EPS) * g


def _rmsnorm_kernel(x_ref, g_ref, o_ref):
    o_ref[...] = _rms(x_ref[...], g_ref[...]).astype(o_ref.dtype)


def _rmsnorm(x, g, out_dtype):
    m, d = x.shape
    return pl.pallas_call(
        _rmsnorm_kernel,
        grid=(m // NORM_ROWS,),
        in_specs=[pl.BlockSpec((NORM_ROWS, d), lambda i: (i, 0)),
                  pl.BlockSpec((1, d), lambda i: (0, 0))],
        out_specs=pl.BlockSpec((NORM_ROWS, d), lambda i: (i, 0)),
        out_shape=jax.ShapeDtypeStruct((m, d), out_dtype),
        compiler_params=_params("arbitrary"),
        name="rmsnorm",
    )(x, g)


def _final_norm_kernel(n_prompt_tiles, x_ref, g_ref, op_ref, os_ref):
    y = _rms(x_ref[...], g_ref[...])
    i = pl.program_id(0)

    @pl.when(i < n_prompt_tiles)
    def _():
        op_ref[...] = y

    @pl.when(i >= n_prompt_tiles)
    def _():
        os_ref[...] = y


def _final_norm(x, g, m_prompt):
    m, d = x.shape
    npt = m_prompt // NORM_ROWS
    return pl.pallas_call(
        functools.partial(_final_norm_kernel, npt),
        grid=(m // NORM_ROWS,),
        in_specs=[pl.BlockSpec((NORM_ROWS, d), lambda i: (i, 0)),
                  pl.BlockSpec((1, d), lambda i: (0, 0))],
        out_specs=[pl.BlockSpec((NORM_ROWS, d), lambda i: (jnp.minimum(i, npt - 1), 0)),
                   pl.BlockSpec((NORM_ROWS, d), lambda i: (jnp.maximum(i - npt, 0), 0))],
        out_shape=[jax.ShapeDtypeStruct((m_prompt, d), F32),
                   jax.ShapeDtypeStruct((m - m_prompt, d), F32)],
        compiler_params=_params("arbitrary"),
        name="final_norm",
    )(x, g)


def _proj_kernel(act, h_ref, w_ref, o_ref, wb_ref):
    @pl.when(pl.program_id(1) == 0)
    def _():
        wb_ref[...] = w_ref[...].astype(BF16)

    p = jnp.dot(h_ref[...], wb_ref[...], preferred_element_type=F32)
    if act == "silu":
        p = p * jax.nn.sigmoid(p)
    elif act == "sigmoid":
        p = jax.nn.sigmoid(p)
    o_ref[...] = p.astype(o_ref.dtype)


def _proj(h, w_in, layer, col_block_of, n_col_blocks, act, out_dtype):
    m, k = h.shape
    return pl.pallas_call(
        functools.partial(_proj_kernel, act),
        grid=(n_col_blocks, m // ROW_TILE),
        in_specs=[pl.BlockSpec((ROW_TILE, k), lambda j, i: (i, 0)),
                  pl.BlockSpec((None, k, COL_TILE), lambda j, i: (layer, 0, col_block_of(j)))],
        out_specs=pl.BlockSpec((ROW_TILE, COL_TILE), lambda j, i: (i, j)),
        out_shape=jax.ShapeDtypeStruct((m, n_col_blocks * COL_TILE), out_dtype),
        scratch_shapes=[pltpu.VMEM((k, COL_TILE), BF16)],
        compiler_params=_params("arbitrary", "arbitrary"),
        name="in_proj_" + act,
    )(h, w_in)


def _tril_bf16(w):
    n = w.shape[-1]
    row = lax.broadcasted_iota(jnp.int32, (n, n), 0)
    col = lax.broadcasted_iota(jnp.int32, (n, n), 1)
    return jnp.where(col <= row, w, 0.0).astype(BF16)


def _mix_prompt_kernel(a_ref, halo_ref, u_ref, v_ref, ga_ref, gb_ref, pw_ref, ps_ref, sg_ref,
                       sw_ref, sbt_ref, ya_ref, yb_ref, st_ref, f_ref):
    s = pl.program_id(1)
    rows, width = a_ref.shape
    gdim = width // POOL_GROUPS
    hdim = width // SGU_HEADS

    f_ref[0:HALO, :] = jnp.where(s == 0, 0.0, halo_ref[...])
    f_ref[HALO:, :] = a_ref[...]
    pos = s * rows + lax.broadcasted_iota(jnp.int32, (rows, 1), 0)
    for g, w in enumerate(POOL_WINDOWS):
        cs = slice(g * gdim, (g + 1) * gdim)
        tok = a_ref[:, cs]
        acc = tok
        for k in range(1, w):
            acc = acc + f_ref[HALO - k:HALO - k + rows, cs]
        cnt = jnp.minimum(pos + 1, w).astype(F32)
        pooled = (acc / cnt - tok).astype(BF16)
        mixed = jnp.dot(pooled, pw_ref[g], preferred_element_type=F32)
        ya_ref[:, cs] = (mixed * ps_ref[:, cs] * ga_ref[:, cs].astype(F32)).astype(BF16)

    @pl.when(s == pl.num_programs(1) - 1)
    def _():
        st_ref[...] = f_ref[HALO + rows - POOL_BUF:HALO + rows, :]

    vn = _rms(v_ref[...], sg_ref[...]).astype(BF16)
    for h in range(SGU_HEADS):
        cs = slice(h * hdim, (h + 1) * hdim)
        wsh = _tril_bf16(sw_ref[h])
        bias = sbt_ref[:, h:h + 1]
        for c in range(rows // CHUNK):
            rs = slice(c * CHUNK, (c + 1) * CHUNK)
            z = jnp.dot(wsh, vn[rs, cs], preferred_element_type=F32) + bias
            yb_ref[rs, cs] = (u_ref[rs, cs] * z * gb_ref[rs, cs].astype(F32)).astype(BF16)


def _mix_prompt(auv, gates, pw, ps, sg, sw, sbt, layer, batch, seq):
    m, width3 = auv.shape
    width = width3 // 3
    tiles = seq // MIX_ROWS
    halo_per_tile = MIX_ROWS // HALO
    row = lambda b, s: b * tiles + s
    return pl.pallas_call(
        _mix_prompt_kernel,
        grid=(batch, tiles),
        in_specs=[
            pl.BlockSpec((MIX_ROWS, width), lambda b, s: (row(b, s), 0)),
            pl.BlockSpec((HALO, width), lambda b, s: (jnp.maximum(row(b, s) * halo_per_tile - 1, 0), 0)),
            pl.BlockSpec((MIX_ROWS, width), lambda b, s: (row(b, s), 1)),
            pl.BlockSpec((MIX_ROWS, width), lambda b, s: (row(b, s), 2)),
            pl.BlockSpec((MIX_ROWS, width), lambda b, s: (row(b, s), 0)),
            pl.BlockSpec((MIX_ROWS, width), lambda b, s: (row(b, s), 1)),
            pl.BlockSpec((None,) + pw.shape[1:], lambda b, s: (layer, 0, 0, 0)),
            pl.BlockSpec((None, 1, width), lambda b, s: (layer, 0, 0)),
            pl.BlockSpec((None, 1, width), lambda b, s: (layer, 0, 0)),
            pl.BlockSpec((None,) + sw.shape[1:], lambda b, s: (layer, 0, 0, 0)),
            pl.BlockSpec((None,) + sbt.shape[1:], lambda b, s: (layer, 0, 0)),
        ],
        out_specs=[
            pl.BlockSpec((MIX_ROWS, width), lambda b, s: (row(b, s), 0)),
            pl.BlockSpec((MIX_ROWS, width), lambda b, s: (row(b, s), 0)),
            pl.BlockSpec((None, POOL_BUF, width), lambda b, s: (b, 0, 0)),
        ],
        out_shape=[jax.ShapeDtypeStruct((m, width), BF16),
                   jax.ShapeDtypeStruct((m, width), BF16),
                   jax.ShapeDtypeStruct((batch, POOL_BUF, width), F32)],
        scratch_shapes=[pltpu.VMEM((HALO + MIX_ROWS, width), F32)],
        compiler_params=_params("arbitrary", "arbitrary"),
        name="mix_prompt",
    )(auv, auv, auv, auv, gates, gates, pw, ps, sg, sw, sbt)


def _mix_sample_kernel(a_ref, u_ref, v_ref, ga_ref, gb_ref, buf_ref, pw_ref, ps_ref, sg_ref,
                       sw_ref, ya_any, yb_any, ya_ref, yb_ref, nb_ref, vn_ref):
    del ya_any, yb_any
    rows, width = a_ref.shape
    nseq = nb_ref.shape[0]
    steps = rows // nseq
    gdim = width // POOL_GROUPS
    slab = lambda t: slice(t * nseq, (t + 1) * nseq)
    lanes = lambda r: slice(r * width, (r + 1) * width)

    def full(i, cs):
        if i < POOL_BUF:
            return buf_ref[:, i * width + cs.start:i * width + cs.stop]
        return a_ref[slab(i - POOL_BUF), cs]

    for r in range(POOL_BUF):
        nb_ref[:, lanes(r)] = full(steps + r, slice(0, width))

    for t in range(steps):
        for g, w in enumerate(POOL_WINDOWS):
            cs = slice(g * gdim, (g + 1) * gdim)
            tok = full(POOL_BUF + t, cs)
            acc = tok
            for k in range(1, w):
                acc = acc + full(POOL_BUF + t - k, cs)
            cnt = float(min(PAST_LEN + t + 1, w))
            pooled = (acc / cnt - tok).astype(BF16)
            mixed = jnp.dot(pooled, pw_ref[g], preferred_element_type=F32)
            ya_ref[slab(t), cs] = (mixed * ps_ref[:, cs] * ga_ref[slab(t), cs].astype(F32)).astype(BF16)

    for t in range(steps):
        vn_ref[:, lanes(t)] = _rms(v_ref[slab(t), :], sg_ref[...])
    for t in range(steps):
        z = sw_ref[steps, t:t + 1, :]
        for s in range(t + 1):
            z = z + sw_ref[s, t:t + 1, :] * vn_ref[:, lanes(s)].astype(BF16).astype(F32)
        yb_ref[slab(t), :] = (u_ref[slab(t), :] * z * gb_ref[slab(t), :].astype(F32)).astype(BF16)


def _mix_sample(auv, gates, ya, yb, buf, pw, ps, sg, swx, layer, m_prompt, steps):
    m, width3 = auv.shape
    width = width3 // 3
    nseq = buf.shape[1]
    rows = MIX_SEQS * steps
    first = m_prompt // rows
    row = lambda i: first + i
    return pl.pallas_call(
        _mix_sample_kernel,
        grid=(nseq // MIX_SEQS,),
        in_specs=[
            pl.BlockSpec((rows, width), lambda i: (row(i), 0)),
            pl.BlockSpec((rows, width), lambda i: (row(i), 1)),
            pl.BlockSpec((rows, width), lambda i: (row(i), 2)),
            pl.BlockSpec((rows, width), lambda i: (row(i), 0)),
            pl.BlockSpec((rows, width), lambda i: (row(i), 1)),
            pl.BlockSpec((None, MIX_SEQS, POOL_BUF * width), lambda i: (layer, i, 0)),
            pl.BlockSpec((None,) + pw.shape[1:], lambda i: (layer, 0, 0, 0)),
            pl.BlockSpec((None, 1, width), lambda i: (layer, 0, 0)),
            pl.BlockSpec((None, 1, width), lambda i: (layer, 0, 0)),
            pl.BlockSpec((None,) + swx.shape[1:], lambda i: (layer, 0, 0, 0)),
            pl.BlockSpec(memory_space=pl.ANY),
            pl.BlockSpec(memory_space=pl.ANY),
        ],
        out_specs=[
            pl.BlockSpec((rows, width), lambda i: (row(i), 0)),
            pl.BlockSpec((rows, width), lambda i: (row(i), 0)),
            pl.BlockSpec((MIX_SEQS, POOL_BUF * width), lambda i: (i, 0)),
            pl.BlockSpec((MIX_SEQS, steps * width), lambda i: (i, 0)),
        ],
        out_shape=[jax.ShapeDtypeStruct(ya.shape, ya.dtype),
                   jax.ShapeDtypeStruct(yb.shape, yb.dtype),
                   jax.ShapeDtypeStruct((nseq, POOL_BUF * width), F32),
                   jax.ShapeDtypeStruct((nseq, steps * width), F32)],
        input_output_aliases={10: 0, 11: 1},
        compiler_params=_params("arbitrary"),
        name="mix_sample",
    )(auv, auv, auv, gates, gates, buf, pw, ps, sg, swx, ya, yb)


def _merge_kernel(ya_ref, yb_ref, wa_ref, wb_ref, sa_ref, sb_ref, o_ref, wab_ref, wbb_ref):
    @pl.when(pl.program_id(1) == 0)
    def _():
        wab_ref[...] = wa_ref[...].astype(BF16)
        wbb_ref[...] = wb_ref[...].astype(BF16)

    pa = jnp.dot(ya_ref[...], wab_ref[...], preferred_element_type=F32)
    pb = jnp.dot(yb_ref[...], wbb_ref[...], preferred_element_type=F32)
    o_ref[...] = (sa_ref[...].astype(F32) * pa + sb_ref[...].astype(F32) * pb).astype(o_ref.dtype)


def _merge(ya, yb, wa, wb, sel, layer):
    m, k = ya.shape
    n = wa.shape[-1]
    nb = n // COL_TILE
    return pl.pallas_call(
        _merge_kernel,
        grid=(nb, m // ROW_TILE),
        in_specs=[pl.BlockSpec((ROW_TILE, k), lambda j, i: (i, 0)),
                  pl.BlockSpec((ROW_TILE, k), lambda j, i: (i, 0)),
                  pl.BlockSpec((None, k, COL_TILE), lambda j, i: (layer, 0, j)),
                  pl.BlockSpec((None, k, COL_TILE), lambda j, i: (layer, 0, j)),
                  pl.BlockSpec((ROW_TILE, COL_TILE), lambda j, i: (i, j)),
                  pl.BlockSpec((ROW_TILE, COL_TILE), lambda j, i: (i, j + nb))],
        out_specs=pl.BlockSpec((ROW_TILE, COL_TILE), lambda j, i: (i, j)),
        out_shape=jax.ShapeDtypeStruct((m, n), BF16),
        scratch_shapes=[pltpu.VMEM((k, COL_TILE), BF16), pltpu.VMEM((k, COL_TILE), BF16)],
        compiler_params=_params("arbitrary", "arbitrary"),
        name="branch_merge",
    )(ya, yb, wa, wb, sel, sel)


def _out_proj_kernel(mg_ref, w_ref, x_ref, o_ref, wb_ref):
    @pl.when(pl.program_id(1) == 0)
    def _():
        wb_ref[...] = w_ref[...].astype(BF16)

    o_ref[...] = x_ref[...] + jnp.dot(mg_ref[...], wb_ref[...], preferred_element_type=F32)


def _out_proj(merged, w_out, x, layer):
    m, k = merged.shape
    n = w_out.shape[-1]
    return pl.pallas_call(
        _out_proj_kernel,
        grid=(n // COL_TILE, m // ROW_TILE),
        in_specs=[pl.BlockSpec((ROW_TILE, k), lambda j, i: (i, 0)),
                  pl.BlockSpec((None, k, COL_TILE), lambda j, i: (layer, 0, j)),
                  pl.BlockSpec((ROW_TILE, COL_TILE), lambda j, i: (i, j))],
        out_specs=pl.BlockSpec((ROW_TILE, COL_TILE), lambda j, i: (i, j)),
        out_shape=jax.ShapeDtypeStruct((m, n), F32),
        scratch_shapes=[pltpu.VMEM((k, COL_TILE), BF16)],
        compiler_params=_params("arbitrary", "arbitrary"),
        name="out_proj",
    )(merged, w_out, x)


def kernel(x_prompt, x_sample, state_pool, norm_g, w_in, pool_w, pool_scale, sgu_norm_g, sgu_w, sgu_b,
           w_branch_a, w_branch_b, w_out, final_norm_g):
    batch, seq, d = x_prompt.shape
    nseq, steps, _ = x_sample.shape
    depth = w_in.shape[0]
    width = pool_scale.shape[-1]
    hdim = width // SGU_HEADS
    m_prompt = batch * seq
    assert PAST_LEN % CHUNK == 0 and steps <= CHUNK
    wt = width // COL_TILE
    ident_cols = lambda j: jnp.where(j < wt, j, j + wt)
    silu_cols = lambda j: jnp.where(j < wt, j + wt, j + 3 * wt)
    sigm_cols = lambda j: j + 5 * wt

    seq_tiles = nseq // MIX_SEQS
    xs = jnp.swapaxes(x_sample.reshape(seq_tiles, MIX_SEQS, steps, d), 1, 2).reshape(nseq * steps, d)
    x = jnp.concatenate([x_prompt.reshape(m_prompt, d), xs], axis=0)
    pw = pool_w.astype(BF16)
    ps = pool_scale.reshape(depth, 1, width)
    sg = sgu_norm_g.reshape(depth, 1, width)
    sbt = jnp.swapaxes(sgu_b, 1, 2)
    sw_small = sgu_w[:, :, :steps, :steps].astype(BF16).astype(F32)
    swx = jnp.repeat(jnp.transpose(sw_small, (0, 3, 2, 1)), hdim, axis=-1)
    bx = jnp.repeat(jnp.swapaxes(sgu_b[:, :, :steps], 1, 2), hdim, axis=-1)[:, None]
    swx = jnp.concatenate([swx, bx], axis=1)
    buf_all = state_pool.reshape(depth, nseq, POOL_BUF * width)

    pool_p, pool_s, vn_s = [], [], []
    for l in range(depth):
        h = _rmsnorm(x, norm_g[l].reshape(1, d), BF16)
        auv = _proj(h, w_in, l, ident_cols, 3 * wt, "none", F32)
        gates = _proj(h, w_in, l, silu_cols, 2 * wt, "silu", BF16)
        sel = _proj(h, w_in, l, sigm_cols, 2 * (d // COL_TILE), "sigmoid", BF16)
        ya, yb, st_p = _mix_prompt(auv, gates, pw, ps, sg, sgu_w, sbt, l, batch, seq)
        ya, yb, st_s, vn = _mix_sample(auv, gates, ya, yb, buf_all, pw, ps, sg, swx, l, m_prompt, steps)
        merged = _merge(ya, yb, w_branch_a, w_branch_b, sel, l)
        x = _out_proj(merged, w_out, x, l)
        pool_p.append(st_p)
        pool_s.append(st_s.reshape(nseq, POOL_BUF, width))
        vn_s.append(vn.reshape(nseq, steps, width))

    y_p, y_s = _final_norm(x, final_norm_g.reshape(1, d), m_prompt)
    y_s = jnp.swapaxes(y_s.reshape(seq_tiles, steps, MIX_SEQS, d), 1, 2)
    return (y_p.reshape(batch, seq, d), y_s.reshape(nseq, steps, d),
            jnp.stack(pool_p, axis=0), jnp.stack(pool_s, axis=0), jnp.stack(vn_s, axis=0))
```

```python
import functools

import jax
import jax.numpy as jnp
from jax import lax
from jax.experimental import pallas as pl
from jax.experimental.pallas import tpu as pltpu

F32 = jnp.float32
BF16 = jnp.bfloat16

POOL_WINDOWS = (2, 4, 8, 16)
POOL_GROUPS = len(POOL_WINDOWS)
POOL_BUF = max(POOL_WINDOWS) - 1
HALO = POOL_BUF + 1
SGU_HEADS = 8
CHUNK = 128
EPS = 1e-6
PAST_LEN = 16384

V7X_VMEM_LIMIT_BYTES = 60 * 1024 * 1024

ROW_TILE = 1024
COL_TILE = 1024
W_CHUNK = 512
NORM_ROWS = 512
MIX_ROWS = 256
MIX_SEQS = 32


def _params(*semantics):
    return pltpu.CompilerParams(dimension_semantics=semantics,
                                vmem_limit_bytes=V7X_VMEM_LIMIT_BYTES)


def _rms(x, g):
    ms = jnp.mean(x * x, axis=-1, keepdims=True)
    return x * lax.rsqrt(ms + EPS) * g


def _rmsnorm_kernel(x_ref, g_ref, o_ref):
    o_ref[...] = _rms(x_ref[...], g_ref[...]).astype(o_ref.dtype)


def _rmsnorm(x, g, out_dtype):
    m, d = x.shape
    return pl.pallas_call(
        _rmsnorm_kernel,
        grid=(m // NORM_ROWS,),
        in_specs=[pl.BlockSpec((NORM_ROWS, d), lambda i: (i, 0)),
                  pl.BlockSpec((1, d), lambda i: (0, 0))],
        out_specs=pl.BlockSpec((NORM_ROWS, d), lambda i: (i, 0)),
        out_shape=jax.ShapeDtypeStruct((m, d), out_dtype),
        compiler_params=_params("arbitrary"),
        name="rmsnorm",
    )(x, g)


def _final_norm_kernel(n_prompt_tiles, x_ref, g_ref, op_ref, os_ref):
    y = _rms(x_ref[...], g_ref[...])
    i = pl.program_id(0)

    @pl.when(i < n_prompt_tiles)
    def _():
        op_ref[...] = y

    @pl.when(i >= n_prompt_tiles)
    def _():
        os_ref[...] = y


def _final_norm(x, g, m_prompt):
    m, d = x.shape
    npt = m_prompt // NORM_ROWS
    return pl.pallas_call(
        functools.partial(_final_norm_kernel, npt),
        grid=(m // NORM_ROWS,),
        in_specs=[pl.BlockSpec((NORM_ROWS, d), lambda i: (i, 0)),
                  pl.BlockSpec((1, d), lambda i: (0, 0))],
        out_specs=[pl.BlockSpec((NORM_ROWS, d), lambda i: (jnp.minimum(i, npt - 1), 0)),
                   pl.BlockSpec((NORM_ROWS, d), lambda i: (jnp.maximum(i - npt, 0), 0))],
        out_shape=[jax.ShapeDtypeStruct((m_prompt, d), F32),
                   jax.ShapeDtypeStruct((m - m_prompt, d), F32)],
        compiler_params=_params("arbitrary"),
        name="final_norm",
    )(x, g)


def _weight_scratch(k):
    return [pltpu.VMEM((2, k, COL_TILE), BF16),
            pltpu.VMEM((2, W_CHUNK, COL_TILE), F32),
            pltpu.SemaphoreType.DMA((2,))]


def _weight_pipeline(chunk_src, n_chunks, wb_ref, ring_ref, sem_ref):
    j, i = pl.program_id(0), pl.program_id(1)

    def copy(c, jt):
        return pltpu.make_async_copy(chunk_src(c, jt), ring_ref.at[c % 2], sem_ref.at[c % 2])

    def cast(c, half):
        wb_ref[half, c * W_CHUNK:(c + 1) * W_CHUNK, :] = ring_ref[c % 2].astype(BF16)

    @pl.when((j == 0) & (i == 0))
    def _():
        copy(0, 0).start()
        for c in range(n_chunks):
            if c + 1 < n_chunks:
                copy(c + 1, 0).start()
            copy(c, 0).wait()
            cast(c, 0)

    has_next = j + 1 < pl.num_programs(0)
    for c in range(n_chunks):
        @pl.when(has_next & (i == c + 1))
        def _():
            copy(c, j + 1).wait()
            cast(c, (j + 1) % 2)

        @pl.when(has_next & (i == c))
        def _():
            copy(c, j + 1).start()

    return j % 2


def _col_window(w_hbm, layer, c, col_block):
    col0 = col_block * COL_TILE
    if not isinstance(col0, int):
        col0 = pl.multiple_of(col0, COL_TILE)
    return w_hbm.at[layer, pl.ds(c * W_CHUNK, W_CHUNK), pl.ds(col0, COL_TILE)]


def _proj_kernel(act, layer, col_block_of, h_ref, w_hbm, o_ref, wb_ref, ring_ref, sem_ref):
    k = h_ref.shape[1]
    cur = _weight_pipeline(lambda c, jt: _col_window(w_hbm, layer, c, col_block_of(jt)),
                           k // W_CHUNK, wb_ref, ring_ref, sem_ref)
    p = jnp.dot(h_ref[...], wb_ref[cur], preferred_element_type=F32)
    if act == "silu":
        p = p * jax.nn.sigmoid(p)
    elif act == "sigmoid":
        p = jax.nn.sigmoid(p)
    o_ref[...] = p.astype(o_ref.dtype)


def _proj(h, w_in, layer, col_block_of, n_col_blocks, act, out_dtype):
    m, k = h.shape
    assert m // ROW_TILE >= k // W_CHUNK + 1
    return pl.pallas_call(
        functools.partial(_proj_kernel, act, layer, col_block_of),
        grid=(n_col_blocks, m // ROW_TILE),
        in_specs=[pl.BlockSpec((ROW_TILE, k), lambda j, i: (i, 0)),
                  pl.BlockSpec(memory_space=pl.ANY)],
        out_specs=pl.BlockSpec((ROW_TILE, COL_TILE), lambda j, i: (i, j)),
        out_shape=jax.ShapeDtypeStruct((m, n_col_blocks * COL_TILE), out_dtype),
        scratch_shapes=_weight_scratch(k),
        compiler_params=_params("arbitrary", "arbitrary"),
        name="in_proj_" + act,
    )(h, w_in)


def _tril_bf16(w):
    n = w.shape[-1]
    row = lax.broadcasted_iota(jnp.int32, (n, n), 0)
    col = lax.broadcasted_iota(jnp.int32, (n, n), 1)
    return jnp.where(col <= row, w, 0.0).astype(BF16)


def _mix_prompt_kernel(a_ref, halo_ref, u_ref, v_ref, ga_ref, gb_ref, pw_ref, ps_ref, sg_ref,
                       sw_ref, sbt_ref, ya_ref, yb_ref, st_ref, f_ref):
    s = pl.program_id(1)
    rows, width = a_ref.shape
    gdim = width // POOL_GROUPS
    hdim = width // SGU_HEADS

    f_ref[0:HALO, :] = jnp.where(s == 0, 0.0, halo_ref[...])
    f_ref[HALO:, :] = a_ref[...]
    pos = s * rows + lax.broadcasted_iota(jnp.int32, (rows, 1), 0)
    for g, w in enumerate(POOL_WINDOWS):
        cs = slice(g * gdim, (g + 1) * gdim)
        tok = a_ref[:, cs]
        acc = tok
        for k in range(1, w):
            acc = acc + f_ref[HALO - k:HALO - k + rows, cs]
        cnt = jnp.minimum(pos + 1, w).astype(F32)
        pooled = (acc / cnt - tok).astype(BF16)
        mixed = jnp.dot(pooled, pw_ref[g], preferred_element_type=F32)
        ya_ref[:, cs] = (mixed * ps_ref[:, cs] * ga_ref[:, cs].astype(F32)).astype(BF16)

    @pl.when(s == pl.num_programs(1) - 1)
    def _():
        st_ref[...] = f_ref[HALO + rows - POOL_BUF:HALO + rows, :]

    vn = _rms(v_ref[...], sg_ref[...]).astype(BF16)
    for h in range(SGU_HEADS):
        cs = slice(h * hdim, (h + 1) * hdim)
        wsh = _tril_bf16(sw_ref[h])
        bias = sbt_ref[:, h:h + 1]
        for c in range(rows // CHUNK):
            rs = slice(c * CHUNK, (c + 1) * CHUNK)
            z = jnp.dot(wsh, vn[rs, cs], preferred_element_type=F32) + bias
            yb_ref[rs, cs] = (u_ref[rs, cs] * z * gb_ref[rs, cs].astype(F32)).astype(BF16)


def _mix_prompt(auv, gates, pw, ps, sg, sw, sbt, layer, batch, seq):
    m, width3 = auv.shape
    width = width3 // 3
    tiles = seq // MIX_ROWS
    halo_per_tile = MIX_ROWS // HALO
    row = lambda b, s: b * tiles + s
    return pl.pallas_call(
        _mix_prompt_kernel,
        grid=(batch, tiles),
        in_specs=[
            pl.BlockSpec((MIX_ROWS, width), lambda b, s: (row(b, s), 0)),
            pl.BlockSpec((HALO, width), lambda b, s: (jnp.maximum(row(b, s) * halo_per_tile - 1, 0), 0)),
            pl.BlockSpec((MIX_ROWS, width), lambda b, s: (row(b, s), 1)),
            pl.BlockSpec((MIX_ROWS, width), lambda b, s: (row(b, s), 2)),
            pl.BlockSpec((MIX_ROWS, width), lambda b, s: (row(b, s), 0)),
            pl.BlockSpec((MIX_ROWS, width), lambda b, s: (row(b, s), 1)),
            pl.BlockSpec((None,) + pw.shape[1:], lambda b, s: (layer, 0, 0, 0)),
            pl.BlockSpec((None, 1, width), lambda b, s: (layer, 0, 0)),
            pl.BlockSpec((None, 1, width), lambda b, s: (layer, 0, 0)),
            pl.BlockSpec((None,) + sw.shape[1:], lambda b, s: (layer, 0, 0, 0)),
            pl.BlockSpec((None,) + sbt.shape[1:], lambda b, s: (layer, 0, 0)),
        ],
        out_specs=[
            pl.BlockSpec((MIX_ROWS, width), lambda b, s: (row(b, s), 0)),
            pl.BlockSpec((MIX_ROWS, width), lambda b, s: (row(b, s), 0)),
            pl.BlockSpec((None, POOL_BUF, width), lambda b, s: (b, 0, 0)),
        ],
        out_shape=[jax.ShapeDtypeStruct((m, width), BF16),
                   jax.ShapeDtypeStruct((m, width), BF16),
                   jax.ShapeDtypeStruct((batch, POOL_BUF, width), F32)],
        scratch_shapes=[pltpu.VMEM((HALO + MIX_ROWS, width), F32)],
        compiler_params=_params("arbitrary", "arbitrary"),
        name="mix_prompt",
    )(auv, auv, auv, auv, gates, gates, pw, ps, sg, sw, sbt)


def _mix_sample_kernel(n_aliased, a_ref, u_ref, v_ref, ga_ref, gb_ref, buf_ref, pw_ref, ps_ref, sg_ref,
                       sw_ref, *refs):
    ya_ref, yb_ref, nb_ref, vn_ref = refs[n_aliased:]
    rows, width = a_ref.shape
    nseq = nb_ref.shape[0]
    steps = rows // nseq
    gdim = width // POOL_GROUPS
    slab = lambda t: slice(t * nseq, (t + 1) * nseq)

    def full(i, cs):
        if i < POOL_BUF:
            return buf_ref[:, i, cs]
        return a_ref[slab(i - POOL_BUF), cs]

    for r in range(POOL_BUF):
        nb_ref[:, r, :] = full(steps + r, slice(0, width))

    for t in range(steps):
        for g, w in enumerate(POOL_WINDOWS):
            cs = slice(g * gdim, (g + 1) * gdim)
            tok = full(POOL_BUF + t, cs)
            acc = tok
            for k in range(1, w):
                acc = acc + full(POOL_BUF + t - k, cs)
            cnt = float(min(PAST_LEN + t + 1, w))
            pooled = (acc / cnt - tok).astype(BF16)
            mixed = jnp.dot(pooled, pw_ref[g], preferred_element_type=F32)
            ya_ref[slab(t), cs] = (mixed * ps_ref[:, cs] * ga_ref[slab(t), cs].astype(F32)).astype(BF16)

    vn = [_rms(v_ref[slab(t), :], sg_ref[...]) for t in range(steps)]
    for t in range(steps):
        vn_ref[:, t, :] = vn[t]
    vnb = [x.astype(BF16).astype(F32) for x in vn]
    for t in range(steps):
        z = sw_ref[steps, t:t + 1, :]
        for s in range(t + 1):
            z = z + sw_ref[s, t:t + 1, :] * vnb[s]
        yb_ref[slab(t), :] = (u_ref[slab(t), :] * z * gb_ref[slab(t), :].astype(F32)).astype(BF16)


def _mix_sample(auv, gates, ya, yb, state_pool, carried, pw, ps, sg, swx, layer, m_prompt, steps):
    m, width3 = auv.shape
    width = width3 // 3
    depth, nseq = state_pool.shape[:2]
    rows = MIX_SEQS * steps
    first = m_prompt // rows
    row = lambda i: first + i
    any_spec = pl.BlockSpec(memory_space=pl.ANY)
    aliased = (ya, yb) + (tuple(carried) if carried is not None else ())
    n_in = 10
    return pl.pallas_call(
        functools.partial(_mix_sample_kernel, len(aliased)),
        grid=(nseq // MIX_SEQS,),
        in_specs=[
            pl.BlockSpec((rows, width), lambda i: (row(i), 0)),
            pl.BlockSpec((rows, width), lambda i: (row(i), 1)),
            pl.BlockSpec((rows, width), lambda i: (row(i), 2)),
            pl.BlockSpec((rows, width), lambda i: (row(i), 0)),
            pl.BlockSpec((rows, width), lambda i: (row(i), 1)),
            pl.BlockSpec((None, MIX_SEQS, POOL_BUF, width), lambda i: (layer, i, 0, 0)),
            pl.BlockSpec((None,) + pw.shape[1:], lambda i: (layer, 0, 0, 0)),
            pl.BlockSpec((None, 1, width), lambda i: (layer, 0, 0)),
            pl.BlockSpec((None, 1, width), lambda i: (layer, 0, 0)),
            pl.BlockSpec((None,) + swx.shape[1:], lambda i: (layer, 0, 0, 0)),
        ] + [any_spec] * len(aliased),
        out_specs=[
            pl.BlockSpec((rows, width), lambda i: (row(i), 0)),
            pl.BlockSpec((rows, width), lambda i: (row(i), 0)),
            pl.BlockSpec((None, MIX_SEQS, POOL_BUF, width), lambda i: (layer, i, 0, 0)),
            pl.BlockSpec((None, MIX_SEQS, steps, width), lambda i: (layer, i, 0, 0)),
        ],
        out_shape=[jax.ShapeDtypeStruct(ya.shape, ya.dtype),
                   jax.ShapeDtypeStruct(yb.shape, yb.dtype),
                   jax.ShapeDtypeStruct((depth, nseq, POOL_BUF, width), F32),
                   jax.ShapeDtypeStruct((depth, nseq, steps, width), F32)],
        input_output_aliases={n_in + k: k for k in range(len(aliased))},
        compiler_params=_params("arbitrary"),
        name="mix_sample",
    )(auv, auv, auv, gates, gates, state_pool, pw, ps, sg, swx, *aliased)


def _merge_kernel(layer, ya_ref, yb_ref, wa_hbm, wb_hbm, sa_ref, sb_ref, o_ref, wt_ref, ring_ref, sem_ref):
    k = ya_ref.shape[1]
    per = k // W_CHUNK

    def chunk_src(c, jt):
        if c < per:
            return _col_window(wa_hbm, layer, c, jt)
        return _col_window(wb_hbm, layer, c - per, jt)

    cur = _weight_pipeline(chunk_src, 2 * per, wt_ref, ring_ref, sem_ref)
    pa = jnp.dot(ya_ref[...], wt_ref[cur, 0:k, :], preferred_element_type=F32)
    pb = jnp.dot(yb_ref[...], wt_ref[cur, k:2 * k, :], preferred_element_type=F32)
    o_ref[...] = (sa_ref[...].astype(F32) * pa + sb_ref[...].astype(F32) * pb).astype(o_ref.dtype)


def _merge(ya, yb, wa, wb, sel, layer):
    m, k = ya.shape
    n = wa.shape[-1]
    nb = n // COL_TILE
    assert m // ROW_TILE >= 2 * k // W_CHUNK + 1
    any_spec = pl.BlockSpec(memory_space=pl.ANY)
    return pl.pallas_call(
        functools.partial(_merge_kernel, layer),
        grid=(nb, m // ROW_TILE),
        in_specs=[pl.BlockSpec((ROW_TILE, k), lambda j, i: (i, 0)),
                  pl.BlockSpec((ROW_TILE, k), lambda j, i: (i, 0)),
                  any_spec, any_spec,
                  pl.BlockSpec((ROW_TILE, COL_TILE), lambda j, i: (i, j)),
                  pl.BlockSpec((ROW_TILE, COL_TILE), lambda j, i: (i, j + nb))],
        out_specs=pl.BlockSpec((ROW_TILE, COL_TILE), lambda j, i: (i, j)),
        out_shape=jax.ShapeDtypeStruct((m, n), BF16),
        scratch_shapes=_weight_scratch(2 * k),
        compiler_params=_params("arbitrary", "arbitrary"),
        name="branch_merge",
    )(ya, yb, wa, wb, sel, sel)


def _out_proj_kernel(layer, mg_ref, w_hbm, x_ref, o_ref, wb_ref, ring_ref, sem_ref):
    k = mg_ref.shape[1]
    cur = _weight_pipeline(lambda c, jt: _col_window(w_hbm, layer, c, jt),
                           k // W_CHUNK, wb_ref, ring_ref, sem_ref)
    o_ref[...] = x_ref[...] + jnp.dot(mg_ref[...], wb_ref[cur], preferred_element_type=F32)


def _out_proj(merged, w_out, x, layer):
    m, k = merged.shape
    n = w_out.shape[-1]
    assert m // ROW_TILE >= k // W_CHUNK + 1
    return pl.pallas_call(
        functools.partial(_out_proj_kernel, layer),
        grid=(n // COL_TILE, m // ROW_TILE),
        in_specs=[pl.BlockSpec((ROW_TILE, k), lambda j, i: (i, 0)),
                  pl.BlockSpec(memory_space=pl.ANY),
                  pl.BlockSpec((ROW_TILE, COL_TILE), lambda j, i: (i, j))],
        out_specs=pl.BlockSpec((ROW_TILE, COL_TILE), lambda j, i: (i, j)),
        out_shape=jax.ShapeDtypeStruct((m, n), F32),
        scratch_shapes=_weight_scratch(k),
        compiler_params=_params("arbitrary", "arbitrary"),
        name="out_proj",
    )(merged, w_out, x)


def kernel(x_prompt, x_sample, state_pool, norm_g, w_in, pool_w, pool_scale, sgu_norm_g, sgu_w, sgu_b,
           w_branch_a, w_branch_b, w_out, final_norm_g):
    batch, seq, d = x_prompt.shape
    nseq, steps, _ = x_sample.shape
    depth = w_in.shape[0]
    width = pool_scale.shape[-1]
    hdim = width // SGU_HEADS
    m_prompt = batch * seq
    assert PAST_LEN % CHUNK == 0 and steps <= CHUNK
    wt = width // COL_TILE
    ident_cols = lambda j: jnp.where(j < wt, j, j + wt)
    silu_cols = lambda j: jnp.where(j < wt, j + wt, j + 3 * wt)
    sigm_cols = lambda j: j + 5 * wt

    seq_tiles = nseq // MIX_SEQS
    xs = jnp.swapaxes(x_sample.reshape(seq_tiles, MIX_SEQS, steps, d), 1, 2).reshape(nseq * steps, d)
    x = jnp.concatenate([x_prompt.reshape(m_prompt, d), xs], axis=0)
    pw = pool_w.astype(BF16)
    ps = pool_scale.reshape(depth, 1, width)
    sg = sgu_norm_g.reshape(depth, 1, width)
    sbt = jnp.swapaxes(sgu_b, 1, 2)
    sw_small = sgu_w[:, :, :steps, :steps].astype(BF16).astype(F32)
    swx = jnp.repeat(jnp.transpose(sw_small, (0, 3, 2, 1)), hdim, axis=-1)
    bx = jnp.repeat(jnp.swapaxes(sgu_b[:, :, :steps], 1, 2), hdim, axis=-1)[:, None]
    swx = jnp.concatenate([swx, bx], axis=1)

    pool_p, carried = [], None
    for l in range(depth):
        h = _rmsnorm(x, norm_g[l].reshape(1, d), BF16)
        auv = _proj(h, w_in, l, ident_cols, 3 * wt, "none", F32)
        gates = _proj(h, w_in, l, silu_cols, 2 * wt, "silu", BF16)
        sel = _proj(h, w_in, l, sigm_cols, 2 * (d // COL_TILE), "sigmoid", BF16)
        ya, yb, st_p = _mix_prompt(auv, gates, pw, ps, sg, sgu_w, sbt, l, batch, seq)
        ya, yb, *carried = _mix_sample(auv, gates, ya, yb, state_pool, carried, pw, ps, sg, swx, l,
                                       m_prompt, steps)
        merged = _merge(ya, yb, w_branch_a, w_branch_b, sel, l)
        x = _out_proj(merged, w_out, x, l)
        pool_p.append(st_p)

    y_p, y_s = _final_norm(x, final_norm_g.reshape(1, d), m_prompt)
    y_s = jnp.swapaxes(y_s.reshape(seq_tiles, steps, MIX_SEQS, d), 1, 2)
    return (y_p.reshape(batch, seq, d), y_s.reshape(nseq, steps, d),
            jnp.stack(pool_p, axis=0), carried[0], carried[1])
```

```python
import functools

import jax
import jax.numpy as jnp
from jax import lax
from jax.experimental import pallas as pl
from jax.experimental.pallas import tpu as pltpu

F32 = jnp.float32
BF16 = jnp.bfloat16

POOL_WINDOWS = (2, 4, 8, 16)
POOL_GROUPS = len(POOL_WINDOWS)
POOL_BUF = max(POOL_WINDOWS) - 1
HALO = POOL_BUF + 1
SGU_HEADS = 8
CHUNK = 128
EPS = 1e-6
PAST_LEN = 16384

V7X_VMEM_LIMIT_BYTES = 60 * 1024 * 1024

LANES = 128
ROW_TILE = 1024
COL_TILE = 1024
DOT_COLS = 512
W_CHUNK = 512
SSQ_GROUPS = 4
PREP_ROWS = 256
NORM_ROWS = 512
MIX_ROWS = 256
MIX_SEQS = 32


def _params(*semantics):
    return pltpu.CompilerParams(dimension_semantics=semantics,
                                vmem_limit_bytes=V7X_VMEM_LIMIT_BYTES)


def _rms(x, g):
    ms = jnp.mean(x * x, axis=-1, keepdims=True)
    return x * lax.rsqrt(ms + EPS) * g


def _lane_partials(sq, groups):
    nblk = sq.shape[1] // LANES
    per = nblk // groups
    outs = []
    for q in range(groups):
        acc = sq[:, q * per * LANES:(q * per + 1) * LANES]
        for b in range(q * per + 1, (q + 1) * per):
            acc = acc + sq[:, b * LANES:(b + 1) * LANES]
        outs.append(acc)
    return outs


def _prep_kernel(n_prompt_tiles, xp_ref, xs_ref, g_ref, x_ref, xb_ref, ssq_ref):
    i = pl.program_id(0)

    def emit(src_ref):
        x = src_ref[...]
        x_ref[...] = x
        xb_ref[...] = (x * g_ref[...]).astype(BF16)
        for q, part in enumerate(_lane_partials(x * x, SSQ_GROUPS)):
            ssq_ref[:, q * LANES:(q + 1) * LANES] = part

    @pl.when(i < n_prompt_tiles)
    def _():
        emit(xp_ref)

    @pl.when(i >= n_prompt_tiles)
    def _():
        emit(xs_ref)


def _prep(xp, xs, g):
    mp, d = xp.shape
    m = mp + xs.shape[0]
    npt = mp // PREP_ROWS
    return pl.pallas_call(
        functools.partial(_prep_kernel, npt),
        grid=(m // PREP_ROWS,),
        in_specs=[pl.BlockSpec((PREP_ROWS, d), lambda i: (jnp.minimum(i, npt - 1), 0)),
                  pl.BlockSpec((PREP_ROWS, d), lambda i: (jnp.maximum(i - npt, 0), 0)),
                  pl.BlockSpec((1, d), lambda i: (0, 0))],
        out_specs=[pl.BlockSpec((PREP_ROWS, d), lambda i: (i, 0)),
                   pl.BlockSpec((PREP_ROWS, d), lambda i: (i, 0)),
                   pl.BlockSpec((PREP_ROWS, SSQ_GROUPS * LANES), lambda i: (i, 0))],
        out_shape=[jax.ShapeDtypeStruct((m, d), F32),
                   jax.ShapeDtypeStruct((m, d), BF16),
                   jax.ShapeDtypeStruct((m, SSQ_GROUPS * LANES), F32)],
        compiler_params=_params("arbitrary"),
        name="prep",
    )(xp, xs, g)


def _final_norm_kernel(n_prompt_tiles, x_ref, g_ref, op_ref, os_ref):
    y = _rms(x_ref[...], g_ref[...])
    i = pl.program_id(0)

    @pl.when(i < n_prompt_tiles)
    def _():
        op_ref[...] = y

    @pl.when(i >= n_prompt_tiles)
    def _():
        os_ref[...] = y


def _final_norm(x, g, m_prompt):
    m, d = x.shape
    npt = m_prompt // NORM_ROWS
    return pl.pallas_call(
        functools.partial(_final_norm_kernel, npt),
        grid=(m // NORM_ROWS,),
        in_specs=[pl.BlockSpec((NORM_ROWS, d), lambda i: (i, 0)),
                  pl.BlockSpec((1, d), lambda i: (0, 0))],
        out_specs=[pl.BlockSpec((NORM_ROWS, d), lambda i: (jnp.minimum(i, npt - 1), 0)),
                   pl.BlockSpec((NORM_ROWS, d), lambda i: (jnp.maximum(i - npt, 0), 0))],
        out_shape=[jax.ShapeDtypeStruct((m_prompt, d), F32),
                   jax.ShapeDtypeStruct((m - m_prompt, d), F32)],
        compiler_params=_params("arbitrary"),
        name="final_norm",
    )(x, g)


def _weight_scratch(k):
    return [pltpu.VMEM((2, k, COL_TILE), BF16),
            pltpu.VMEM((2, W_CHUNK, COL_TILE), F32),
            pltpu.SemaphoreType.DMA((2,))]


def _weight_pipeline(chunk_src, n_chunks, wb_ref, ring_ref, sem_ref):
    j, i = pl.program_id(0), pl.program_id(1)

    def copy(c, jt):
        return pltpu.make_async_copy(chunk_src(c, jt), ring_ref.at[c % 2], sem_ref.at[c % 2])

    def cast(c, half):
        wb_ref[half, c * W_CHUNK:(c + 1) * W_CHUNK, :] = ring_ref[c % 2].astype(BF16)

    @pl.when((j == 0) & (i == 0))
    def _():
        copy(0, 0).start()
        for c in range(n_chunks):
            if c + 1 < n_chunks:
                copy(c + 1, 0).start()
            copy(c, 0).wait()
            cast(c, 0)

    has_next = j + 1 < pl.num_programs(0)
    for c in range(n_chunks):
        @pl.when(has_next & (i == c + 1))
        def _():
            copy(c, j + 1).wait()
            cast(c, (j + 1) % 2)

        @pl.when(has_next & (i == c))
        def _():
            copy(c, j + 1).start()

    return j % 2


def _col_window(w_hbm, layer, c, col_block):
    col0 = col_block * COL_TILE
    if not isinstance(col0, int):
        col0 = pl.multiple_of(col0, COL_TILE)
    return w_hbm.at[layer, pl.ds(c * W_CHUNK, W_CHUNK), pl.ds(col0, COL_TILE)]


def _proj_kernel(act, layer, col_block_of, xb_ref, ssq_ref, w_hbm, o_ref, wb_ref, ring_ref, sem_ref):
    k = xb_ref.shape[1]
    cur = _weight_pipeline(lambda c, jt: _col_window(w_hbm, layer, c, col_block_of(jt)),
                           k // W_CHUNK, wb_ref, ring_ref, sem_ref)
    r = lax.rsqrt(jnp.sum(ssq_ref[...], axis=-1, keepdims=True) / k + EPS)
    for q in range(COL_TILE // DOT_COLS):
        cs = slice(q * DOT_COLS, (q + 1) * DOT_COLS)
        p = jnp.dot(xb_ref[...], wb_ref[cur, :, cs], preferred_element_type=F32) * r
        if act == "silu":
            p = p * jax.nn.sigmoid(p)
        elif act == "sigmoid":
            p = jax.nn.sigmoid(p)
        o_ref[:, cs] = p.astype(o_ref.dtype)


def _proj(xb, ssq, w_in, layer, col_block_of, n_col_blocks, act, out_dtype):
    m, k = xb.shape
    assert m // ROW_TILE >= k // W_CHUNK + 1
    return pl.pallas_call(
        functools.partial(_proj_kernel, act, layer, col_block_of),
        grid=(n_col_blocks, m // ROW_TILE),
        in_specs=[pl.BlockSpec((ROW_TILE, k), lambda j, i: (i, 0)),
                  pl.BlockSpec((ROW_TILE, ssq.shape[1]), lambda j, i: (i, 0)),
                  pl.BlockSpec(memory_space=pl.ANY)],
        out_specs=pl.BlockSpec((ROW_TILE, COL_TILE), lambda j, i: (i, j)),
        out_shape=jax.ShapeDtypeStruct((m, n_col_blocks * COL_TILE), out_dtype),
        scratch_shapes=_weight_scratch(k),
        compiler_params=_params("arbitrary", "arbitrary"),
        name="in_proj_" + act,
    )(xb, ssq, w_in)


def _tril_bf16(w):
    n = w.shape[-1]
    row = lax.broadcasted_iota(jnp.int32, (n, n), 0)
    col = lax.broadcasted_iota(jnp.int32, (n, n), 1)
    return jnp.where(col <= row, w, 0.0).astype(BF16)


def _mix_prompt_kernel(a_ref, halo_ref, u_ref, v_ref, ga_ref, gb_ref, pw_ref, ps_ref, sg_ref,
                       sw_ref, sbt_ref, ya_ref, yb_ref, st_ref, f_ref):
    s = pl.program_id(1)
    rows, width = a_ref.shape
    gdim = width // POOL_GROUPS
    hdim = width // SGU_HEADS

    f_ref[0:HALO, :] = jnp.where(s == 0, 0.0, halo_ref[...])
    f_ref[HALO:, :] = a_ref[...]
    pos = s * rows + lax.broadcasted_iota(jnp.int32, (rows, 1), 0)
    for g, w in enumerate(POOL_WINDOWS):
        cs = slice(g * gdim, (g + 1) * gdim)
        tok = a_ref[:, cs]
        acc = tok
        for k in range(1, w):
            acc = acc + f_ref[HALO - k:HALO - k + rows, cs]
        cnt = jnp.minimum(pos + 1, w).astype(F32)
        pooled = (acc / cnt - tok).astype(BF16)
        mixed = jnp.dot(pooled, pw_ref[g], preferred_element_type=F32)
        ya_ref[:, cs] = (mixed * ps_ref[:, cs] * ga_ref[:, cs].astype(F32)).astype(BF16)

    @pl.when(s == pl.num_programs(1) - 1)
    def _():
        st_ref[...] = f_ref[HALO + rows - POOL_BUF:HALO + rows, :]

    vn = _rms(v_ref[...], sg_ref[...]).astype(BF16)
    for h in range(SGU_HEADS):
        cs = slice(h * hdim, (h + 1) * hdim)
        wsh = _tril_bf16(sw_ref[h])
        bias = sbt_ref[:, h:h + 1]
        for c in range(rows // CHUNK):
            rs = slice(c * CHUNK, (c + 1) * CHUNK)
            z = jnp.dot(wsh, vn[rs, cs], preferred_element_type=F32) + bias
            yb_ref[rs, cs] = (u_ref[rs, cs] * z * gb_ref[rs, cs].astype(F32)).astype(BF16)


def _mix_prompt(auv, gates, pw, ps, sg, sw, sbt, layer, batch, seq):
    m, width3 = auv.shape
    width = width3 // 3
    tiles = seq // MIX_ROWS
    halo_per_tile = MIX_ROWS // HALO
    row = lambda b, s: b * tiles + s
    return pl.pallas_call(
        _mix_prompt_kernel,
        grid=(batch, tiles),
        in_specs=[
            pl.BlockSpec((MIX_ROWS, width), lambda b, s: (row(b, s), 0)),
            pl.BlockSpec((HALO, width), lambda b, s: (jnp.maximum(row(b, s) * halo_per_tile - 1, 0), 0)),
            pl.BlockSpec((MIX_ROWS, width), lambda b, s: (row(b, s), 1)),
            pl.BlockSpec((MIX_ROWS, width), lambda b, s: (row(b, s), 2)),
            pl.BlockSpec((MIX_ROWS, width), lambda b, s: (row(b, s), 0)),
            pl.BlockSpec((MIX_ROWS, width), lambda b, s: (row(b, s), 1)),
            pl.BlockSpec((None,) + pw.shape[1:], lambda b, s: (layer, 0, 0, 0)),
            pl.BlockSpec((None, 1, width), lambda b, s: (layer, 0, 0)),
            pl.BlockSpec((None, 1, width), lambda b, s: (layer, 0, 0)),
            pl.BlockSpec((None,) + sw.shape[1:], lambda b, s: (layer, 0, 0, 0)),
            pl.BlockSpec((None,) + sbt.shape[1:], lambda b, s: (layer, 0, 0)),
        ],
        out_specs=[
            pl.BlockSpec((MIX_ROWS, width), lambda b, s: (row(b, s), 0)),
            pl.BlockSpec((MIX_ROWS, width), lambda b, s: (row(b, s), 0)),
            pl.BlockSpec((None, POOL_BUF, width), lambda b, s: (b, 0, 0)),
        ],
        out_shape=[jax.ShapeDtypeStruct((m, width), BF16),
                   jax.ShapeDtypeStruct((m, width), BF16),
                   jax.ShapeDtypeStruct((batch, POOL_BUF, width), F32)],
        scratch_shapes=[pltpu.VMEM((HALO + MIX_ROWS, width), F32)],
        compiler_params=_params("arbitrary", "arbitrary"),
        name="mix_prompt",
    )(auv, auv, auv, auv, gates, gates, pw, ps, sg, sw, sbt)


def _mix_sample_kernel(n_aliased, a_ref, u_ref, v_ref, ga_ref, gb_ref, buf_ref, pw_ref, ps_ref, sg_ref,
                       sw_ref, *refs):
    ya_ref, yb_ref, nb_ref, vn_ref = refs[n_aliased:]
    rows, width = a_ref.shape
    nseq = nb_ref.shape[0]
    steps = rows // nseq
    gdim = width // POOL_GROUPS
    slab = lambda t: slice(t * nseq, (t + 1) * nseq)

    def full(i, cs):
        if i < POOL_BUF:
            return buf_ref[:, i, cs]
        return a_ref[slab(i - POOL_BUF), cs]

    for r in range(POOL_BUF):
        nb_ref[:, r, :] = full(steps + r, slice(0, width))

    for t in range(steps):
        for g, w in enumerate(POOL_WINDOWS):
            cs = slice(g * gdim, (g + 1) * gdim)
            tok = full(POOL_BUF + t, cs)
            acc = tok
            for k in range(1, w):
                acc = acc + full(POOL_BUF + t - k, cs)
            cnt = float(min(PAST_LEN + t + 1, w))
            pooled = (acc / cnt - tok).astype(BF16)
            mixed = jnp.dot(pooled, pw_ref[g], preferred_element_type=F32)
            ya_ref[slab(t), cs] = (mixed * ps_ref[:, cs] * ga_ref[slab(t), cs].astype(F32)).astype(BF16)

    vn = [_rms(v_ref[slab(t), :], sg_ref[...]) for t in range(steps)]
    for t in range(steps):
        vn_ref[:, t, :] = vn[t]
    vnb = [x.astype(BF16).astype(F32) for x in vn]
    for t in range(steps):
        z = sw_ref[steps, t:t + 1, :]
        for s in range(t + 1):
            z = z + sw_ref[s, t:t + 1, :] * vnb[s]
        yb_ref[slab(t), :] = (u_ref[slab(t), :] * z * gb_ref[slab(t), :].astype(F32)).astype(BF16)


def _mix_sample(auv, gates, ya, yb, state_pool, carried, pw, ps, sg, swx, layer, m_prompt, steps):
    m, width3 = auv.shape
    width = width3 // 3
    depth, nseq = state_pool.shape[:2]
    rows = MIX_SEQS * steps
    first = m_prompt // rows
    row = lambda i: first + i
    any_spec = pl.BlockSpec(memory_space=pl.ANY)
    aliased = (ya, yb) + (tuple(carried) if carried is not None else ())
    n_in = 10
    return pl.pallas_call(
        functools.partial(_mix_sample_kernel, len(aliased)),
        grid=(nseq // MIX_SEQS,),
        in_specs=[
            pl.BlockSpec((rows, width), lambda i: (row(i), 0)),
            pl.BlockSpec((rows, width), lambda i: (row(i), 1)),
            pl.BlockSpec((rows, width), lambda i: (row(i), 2)),
            pl.BlockSpec((rows, width), lambda i: (row(i), 0)),
            pl.BlockSpec((rows, width), lambda i: (row(i), 1)),
            pl.BlockSpec((None, MIX_SEQS, POOL_BUF, width), lambda i: (layer, i, 0, 0)),
            pl.BlockSpec((None,) + pw.shape[1:], lambda i: (layer, 0, 0, 0)),
            pl.BlockSpec((None, 1, width), lambda i: (layer, 0, 0)),
            pl.BlockSpec((None, 1, width), lambda i: (layer, 0, 0)),
            pl.BlockSpec((None,) + swx.shape[1:], lambda i: (layer, 0, 0, 0)),
        ] + [any_spec] * len(aliased),
        out_specs=[
            pl.BlockSpec((rows, width), lambda i: (row(i), 0)),
            pl.BlockSpec((rows, width), lambda i: (row(i), 0)),
            pl.BlockSpec((None, MIX_SEQS, POOL_BUF, width), lambda i: (layer, i, 0, 0)),
            pl.BlockSpec((None, MIX_SEQS, steps, width), lambda i: (layer, i, 0, 0)),
        ],
        out_shape=[jax.ShapeDtypeStruct(ya.shape, ya.dtype),
                   jax.ShapeDtypeStruct(yb.shape, yb.dtype),
                   jax.ShapeDtypeStruct((depth, nseq, POOL_BUF, width), F32),
                   jax.ShapeDtypeStruct((depth, nseq, steps, width), F32)],
        input_output_aliases={n_in + k: k for k in range(len(aliased))},
        compiler_params=_params("arbitrary"),
        name="mix_sample",
    )(auv, auv, auv, gates, gates, state_pool, pw, ps, sg, swx, *aliased)


def _merge_kernel(layer, ya_ref, yb_ref, wa_hbm, wb_hbm, sa_ref, sb_ref, o_ref, wt_ref, ring_ref, sem_ref):
    k = ya_ref.shape[1]
    per = k // W_CHUNK

    def chunk_src(c, jt):
        if c < per:
            return _col_window(wa_hbm, layer, c, jt)
        return _col_window(wb_hbm, layer, c - per, jt)

    cur = _weight_pipeline(chunk_src, 2 * per, wt_ref, ring_ref, sem_ref)
    for q in range(COL_TILE // DOT_COLS):
        cs = slice(q * DOT_COLS, (q + 1) * DOT_COLS)
        pa = jnp.dot(ya_ref[...], wt_ref[cur, 0:k, cs], preferred_element_type=F32)
        pb = jnp.dot(yb_ref[...], wt_ref[cur, k:2 * k, cs], preferred_element_type=F32)
        o_ref[:, cs] = (sa_ref[:, cs].astype(F32) * pa + sb_ref[:, cs].astype(F32) * pb).astype(o_ref.dtype)


def _merge(ya, yb, wa, wb, sel, layer):
    m, k = ya.shape
    n = wa.shape[-1]
    nb = n // COL_TILE
    assert m // ROW_TILE >= 2 * k // W_CHUNK + 1
    any_spec = pl.BlockSpec(memory_space=pl.ANY)
    return pl.pallas_call(
        functools.partial(_merge_kernel, layer),
        grid=(nb, m // ROW_TILE),
        in_specs=[pl.BlockSpec((ROW_TILE, k), lambda j, i: (i, 0)),
                  pl.BlockSpec((ROW_TILE, k), lambda j, i: (i, 0)),
                  any_spec, any_spec,
                  pl.BlockSpec((ROW_TILE, COL_TILE), lambda j, i: (i, j)),
                  pl.BlockSpec((ROW_TILE, COL_TILE), lambda j, i: (i, j + nb))],
        out_specs=pl.BlockSpec((ROW_TILE, COL_TILE), lambda j, i: (i, j)),
        out_shape=jax.ShapeDtypeStruct((m, n), BF16),
        scratch_shapes=_weight_scratch(2 * k),
        compiler_params=_params("arbitrary", "arbitrary"),
        name="branch_merge",
    )(ya, yb, wa, wb, sel, sel)


def _out_proj_kernel(layer, feeds_next, mg_ref, w_hbm, x_ref, *refs):
    if feeds_next:
        g_ref, o_ref, xb_ref, ssq_ref = refs[:4]
    else:
        o_ref = refs[0]
    wb_ref, ring_ref, sem_ref = refs[-3:]
    k = mg_ref.shape[1]
    cur = _weight_pipeline(lambda c, jt: _col_window(w_hbm, layer, c, jt),
                           k // W_CHUNK, wb_ref, ring_ref, sem_ref)
    parts = []
    for q in range(COL_TILE // DOT_COLS):
        cs = slice(q * DOT_COLS, (q + 1) * DOT_COLS)
        y = x_ref[:, cs] + jnp.dot(mg_ref[...], wb_ref[cur, :, cs], preferred_element_type=F32)
        o_ref[:, cs] = y
        if feeds_next:
            xb_ref[:, cs] = (y * g_ref[:, cs]).astype(BF16)
            parts += _lane_partials(y * y, 1)
    if feeds_next:
        ssq_ref[...] = functools.reduce(lambda a, b: a + b, parts)


def _out_proj(merged, w_out, x, layer, g_all):
    m, k = merged.shape
    n = w_out.shape[-1]
    assert m // ROW_TILE >= k // W_CHUNK + 1
    feeds_next = g_all is not None
    tile = pl.BlockSpec((ROW_TILE, COL_TILE), lambda j, i: (i, j))
    in_specs = [pl.BlockSpec((ROW_TILE, k), lambda j, i: (i, 0)), pl.BlockSpec(memory_space=pl.ANY), tile]
    args = [merged, w_out, x]
    out_specs, out_shape = [tile], [jax.ShapeDtypeStruct((m, n), F32)]
    if feeds_next:
        assert n // COL_TILE == SSQ_GROUPS
        in_specs.append(pl.BlockSpec((None, 1, COL_TILE), lambda j, i: (layer + 1, 0, j)))
        args.append(g_all)
        out_specs += [tile, pl.BlockSpec((ROW_TILE, LANES), lambda j, i: (i, j))]
        out_shape += [jax.ShapeDtypeStruct((m, n), BF16), jax.ShapeDtypeStruct((m, SSQ_GROUPS * LANES), F32)]
    return pl.pallas_call(
        functools.partial(_out_proj_kernel, layer, feeds_next),
        grid=(n // COL_TILE, m // ROW_TILE),
        in_specs=in_specs,
        out_specs=out_specs,
        out_shape=out_shape,
        scratch_shapes=_weight_scratch(k),
        compiler_params=_params("arbitrary", "arbitrary"),
        name="out_proj",
    )(*args)


def kernel(x_prompt, x_sample, state_pool, norm_g, w_in, pool_w, pool_scale, sgu_norm_g, sgu_w, sgu_b,
           w_branch_a, w_branch_b, w_out, final_norm_g):
    batch, seq, d = x_prompt.shape
    nseq, steps, _ = x_sample.shape
    depth = w_in.shape[0]
    width = pool_scale.shape[-1]
    hdim = width // SGU_HEADS
    m_prompt = batch * seq
    assert PAST_LEN % CHUNK == 0 and steps <= CHUNK
    wt = width // COL_TILE
    ident_cols = lambda j: jnp.where(j < wt, j, j + wt)
    silu_cols = lambda j: jnp.where(j < wt, j + wt, j + 3 * wt)
    sigm_cols = lambda j: j + 5 * wt

    seq_tiles = nseq // MIX_SEQS
    xs = jnp.swapaxes(x_sample.reshape(seq_tiles, MIX_SEQS, steps, d), 1, 2).reshape(nseq * steps, d)
    g_all = norm_g.reshape(depth, 1, d)
    x, xb, ssq = _prep(x_prompt.reshape(m_prompt, d), xs, g_all[0])
    pw = pool_w.astype(BF16)
    ps = pool_scale.reshape(depth, 1, width)
    sg = sgu_norm_g.reshape(depth, 1, width)
    sbt = jnp.swapaxes(sgu_b, 1, 2)
    sw_small = sgu_w[:, :, :steps, :steps].astype(BF16).astype(F32)
    swx = jnp.repeat(jnp.transpose(sw_small, (0, 3, 2, 1)), hdim, axis=-1)
    bx = jnp.repeat(jnp.swapaxes(sgu_b[:, :, :steps], 1, 2), hdim, axis=-1)[:, None]
    swx = jnp.concatenate([swx, bx], axis=1)

    pool_p, carried = [], None
    for l in range(depth):
        auv = _proj(xb, ssq, w_in, l, ident_cols, 3 * wt, "none", F32)
        gates = _proj(xb, ssq, w_in, l, silu_cols, 2 * wt, "silu", BF16)
        sel = _proj(xb, ssq, w_in, l, sigm_cols, 2 * (d // COL_TILE), "sigmoid", BF16)
        ya, yb, st_p = _mix_prompt(auv, gates, pw, ps, sg, sgu_w, sbt, l, batch, seq)
        ya, yb, *carried = _mix_sample(auv, gates, ya, yb, state_pool, carried, pw, ps, sg, swx, l,
                                       m_prompt, steps)
        merged = _merge(ya, yb, w_branch_a, w_branch_b, sel, l)
        if l + 1 < depth:
            x, xb, ssq = _out_proj(merged, w_out, x, l, g_all)
        else:
            x, = _out_proj(merged, w_out, x, l, None)
        pool_p.append(st_p)

    y_p, y_s = _final_norm(x, final_norm_g.reshape(1, d), m_prompt)
    y_s = jnp.swapaxes(y_s.reshape(seq_tiles, steps, MIX_SEQS, d), 1, 2)
    return (y_p.reshape(batch, seq, d), y_s.reshape(nseq, steps, d),
            jnp.stack(pool_p, axis=0), carried[0], carried[1])
```

```python
import functools

import jax
import jax.numpy as jnp
from jax import lax
from jax.experimental import pallas as pl
from jax.experimental.pallas import tpu as pltpu

F32 = jnp.float32
BF16 = jnp.bfloat16

POOL_WINDOWS = (2, 4, 8, 16)
POOL_GROUPS = len(POOL_WINDOWS)
POOL_BUF = max(POOL_WINDOWS) - 1
HALO = POOL_BUF + 1
SGU_HEADS = 8
CHUNK = 128
EPS = 1e-6
PAST_LEN = 16384

V7X_VMEM_LIMIT_BYTES = 60 * 1024 * 1024

LANES = 128
ROW_TILE = 1024
COL_TILE = 1024
DOT_COLS = 512
GATE_DOT_COLS = 256
W_CHUNK = 512
SSQ_GROUPS = 4
PREP_ROWS = 256
NORM_ROWS = 512
MIX_ROWS = 128
MIX_FRONT = 2 * HALO
MIX_SEQS = 32


def _params(*semantics):
    return pltpu.CompilerParams(dimension_semantics=semantics,
                                vmem_limit_bytes=V7X_VMEM_LIMIT_BYTES)


def _rms(x, g):
    ms = jnp.mean(x * x, axis=-1, keepdims=True)
    return x * lax.rsqrt(ms + EPS) * g


def _lane_partials(sq, groups):
    nblk = sq.shape[1] // LANES
    per = nblk // groups
    outs = []
    for q in range(groups):
        acc = sq[:, q * per * LANES:(q * per + 1) * LANES]
        for b in range(q * per + 1, (q + 1) * per):
            acc = acc + sq[:, b * LANES:(b + 1) * LANES]
        outs.append(acc)
    return outs


def _prep_kernel(n_prompt_tiles, xp_ref, xs_ref, g_ref, x_ref, xb_ref, ssq_ref):
    i = pl.program_id(0)

    def emit(src_ref):
        x = src_ref[...]
        x_ref[...] = x
        xb_ref[...] = (x * g_ref[...]).astype(BF16)
        for q, part in enumerate(_lane_partials(x * x, SSQ_GROUPS)):
            ssq_ref[:, q * LANES:(q + 1) * LANES] = part

    @pl.when(i < n_prompt_tiles)
    def _():
        emit(xp_ref)

    @pl.when(i >= n_prompt_tiles)
    def _():
        emit(xs_ref)


def _prep(xp, xs, g):
    mp, d = xp.shape
    m = mp + xs.shape[0]
    npt = mp // PREP_ROWS
    return pl.pallas_call(
        functools.partial(_prep_kernel, npt),
        grid=(m // PREP_ROWS,),
        in_specs=[pl.BlockSpec((PREP_ROWS, d), lambda i: (jnp.minimum(i, npt - 1), 0)),
                  pl.BlockSpec((PREP_ROWS, d), lambda i: (jnp.maximum(i - npt, 0), 0)),
                  pl.BlockSpec((1, d), lambda i: (0, 0))],
        out_specs=[pl.BlockSpec((PREP_ROWS, d), lambda i: (i, 0)),
                   pl.BlockSpec((PREP_ROWS, d), lambda i: (i, 0)),
                   pl.BlockSpec((PREP_ROWS, SSQ_GROUPS * LANES), lambda i: (i, 0))],
        out_shape=[jax.ShapeDtypeStruct((m, d), F32),
                   jax.ShapeDtypeStruct((m, d), BF16),
                   jax.ShapeDtypeStruct((m, SSQ_GROUPS * LANES), F32)],
        compiler_params=_params("arbitrary"),
        name="prep",
    )(xp, xs, g)


def _final_norm_kernel(n_prompt_tiles, x_ref, g_ref, op_ref, os_ref):
    y = _rms(x_ref[...], g_ref[...])
    i = pl.program_id(0)

    @pl.when(i < n_prompt_tiles)
    def _():
        op_ref[...] = y

    @pl.when(i >= n_prompt_tiles)
    def _():
        os_ref[...] = y


def _final_norm(x, g, m_prompt):
    m, d = x.shape
    npt = m_prompt // NORM_ROWS
    return pl.pallas_call(
        functools.partial(_final_norm_kernel, npt),
        grid=(m // NORM_ROWS,),
        in_specs=[pl.BlockSpec((NORM_ROWS, d), lambda i: (i, 0)),
                  pl.BlockSpec((1, d), lambda i: (0, 0))],
        out_specs=[pl.BlockSpec((NORM_ROWS, d), lambda i: (jnp.minimum(i, npt - 1), 0)),
                   pl.BlockSpec((NORM_ROWS, d), lambda i: (jnp.maximum(i - npt, 0), 0))],
        out_shape=[jax.ShapeDtypeStruct((m_prompt, d), F32),
                   jax.ShapeDtypeStruct((m - m_prompt, d), F32)],
        compiler_params=_params("arbitrary"),
        name="final_norm",
    )(x, g)


def _weight_scratch(k):
    return [pltpu.VMEM((k, COL_TILE), BF16),
            pltpu.VMEM((k, COL_TILE), BF16),
            pltpu.VMEM((2, W_CHUNK, COL_TILE), F32),
            pltpu.SemaphoreType.DMA((2,))]


def _weight_pipeline(chunk_src, n_chunks, w_even_ref, w_odd_ref, ring_ref, sem_ref, body):
    j, i = pl.program_id(0), pl.program_id(1)

    def copy(c, jt):
        return pltpu.make_async_copy(chunk_src(c, jt), ring_ref.at[c % 2], sem_ref.at[c % 2])

    @pl.when((j == 0) & (i == 0))
    def _():
        copy(0, 0).start()
        for c in range(n_chunks):
            if c + 1 < n_chunks:
                copy(c + 1, 0).start()
            copy(c, 0).wait()
            w_even_ref[c * W_CHUNK:(c + 1) * W_CHUNK, :] = ring_ref[c % 2].astype(BF16)

    has_next = j + 1 < pl.num_programs(0)
    for c in range(n_chunks):
        @pl.when(has_next & (i == c + 1))
        def _():
            copy(c, j + 1).wait()

        @pl.when(has_next & (i == c))
        def _():
            copy(c, j + 1).start()

    def step(w_ref, next_ref):
        c = jnp.clip(i - 1, 0, n_chunks - 1)
        row0 = pl.multiple_of(c * W_CHUNK, W_CHUNK)
        next_ref[pl.ds(row0, W_CHUNK), :] = ring_ref[(i + 1) % 2].astype(BF16)
        body(w_ref)

    @pl.when(j % 2 == 0)
    def _():
        step(w_even_ref, w_odd_ref)

    @pl.when(j % 2 == 1)
    def _():
        step(w_odd_ref, w_even_ref)


def _col_window(w_hbm, layer, c, col_block):
    col0 = col_block * COL_TILE
    if not isinstance(col0, int):
        col0 = pl.multiple_of(col0, COL_TILE)
    return w_hbm.at[layer, pl.ds(c * W_CHUNK, W_CHUNK), pl.ds(col0, COL_TILE)]


N_MIX_IN = 11


def _proj_kernel(act, layer, col_block_of, emits_r, mix, xb_ref, stat_ref, w_hbm, *refs):
    refs = list(refs)
    mix_in = [refs.pop(0) for _ in range(N_MIX_IN)] if mix else []
    o_ref = refs.pop(0)
    r_ref = refs.pop(0) if emits_r else None
    mix_out = [refs.pop(0) for _ in range(3)] if mix else []
    scratch = refs
    k = xb_ref.shape[1]
    step = pl.program_id(0) * pl.num_programs(1) + pl.program_id(1)

    def row_scale():
        r = lax.rsqrt(jnp.sum(stat_ref[...], axis=-1, keepdims=True) / k + EPS)
        return jnp.broadcast_to(r, (r.shape[0], LANES))

    if emits_r:
        @pl.when(pl.program_id(0) == 0)
        def _():
            r_ref[...] = row_scale()

    if mix:
        tiles_per_seq, n_tiles = mix

        @pl.when(step < n_tiles)
        def _():
            _prompt_mix(step % tiles_per_seq, tiles_per_seq, *mix_in, *mix_out, *scratch[4:])

    def body(w_ref):
        r_lanes = row_scale() if emits_r else stat_ref[...]
        dot_cols = DOT_COLS if act == "none" else GATE_DOT_COLS
        r_cols = jnp.concatenate([r_lanes] * (dot_cols // LANES), axis=1)
        for q in range(COL_TILE // dot_cols):
            cs = slice(q * dot_cols, (q + 1) * dot_cols)
            p = jnp.dot(xb_ref[...], w_ref[:, cs], preferred_element_type=F32) * r_cols
            if act != "none":
                gate = 0.5 * jnp.tanh(0.5 * p) + 0.5
                p = p * gate if act == "silu" else gate
            o_ref[:, cs] = p.astype(o_ref.dtype)

    _weight_pipeline(lambda c, jt: _col_window(w_hbm, layer, c, col_block_of(jt)),
                     k // W_CHUNK, *scratch[:4], body)


def _proj(xb, stat, w_in, layer, col_block_of, n_col_blocks, act, out_dtype, emits_r=False, mix_args=None):
    m, k = xb.shape
    n_rows = m // ROW_TILE
    assert n_rows >= k // W_CHUNK + 1
    tile = pl.BlockSpec((ROW_TILE, COL_TILE), lambda j, i: (i, j))
    in_specs = [pl.BlockSpec((ROW_TILE, k), lambda j, i: (i, 0)),
                pl.BlockSpec((ROW_TILE, stat.shape[1]), lambda j, i: (i, 0)),
                pl.BlockSpec(memory_space=pl.ANY)]
    args = [xb, stat, w_in]
    out_specs, out_shape = [tile], [jax.ShapeDtypeStruct((m, n_col_blocks * COL_TILE), out_dtype)]
    scratch = _weight_scratch(k)
    if emits_r:
        out_specs.append(pl.BlockSpec((ROW_TILE, LANES), lambda j, i: (jnp.where(j == 0, i, n_rows - 1), 0)))
        out_shape.append(jax.ShapeDtypeStruct((m, LANES), F32))
    mix = None
    if mix_args is not None:
        auv, gates, pw, ps, sg, sw, sbt, batch, seq = mix_args
        tiles_per_seq = seq // MIX_ROWS
        mix = (tiles_per_seq, batch * tiles_per_seq)
        assert n_col_blocks * n_rows >= mix[1]
        tile_of = lambda j, i: jnp.minimum(j * n_rows + i, mix[1] - 1)
        mi, margs, mo, mshape, mscratch = _prompt_mix_specs(auv, gates, pw, ps, sg, sw, sbt, layer, batch,
                                                             tiles_per_seq, tile_of)
        in_specs += mi
        args += margs
        out_specs += mo
        out_shape += mshape
        scratch = scratch + mscratch
    return pl.pallas_call(
        functools.partial(_proj_kernel, act, layer, col_block_of, emits_r, mix),
        grid=(n_col_blocks, n_rows),
        in_specs=in_specs,
        out_specs=out_specs,
        out_shape=out_shape,
        scratch_shapes=scratch,
        compiler_params=_params("arbitrary", "arbitrary"),
        name="in_proj_" + act,
    )(*args)


def _tril_bf16(w):
    n = w.shape[-1]
    row = lax.broadcasted_iota(jnp.int32, (n, n), 0)
    col = lax.broadcasted_iota(jnp.int32, (n, n), 1)
    return jnp.where(col <= row, w, 0.0).astype(BF16)


def _prompt_mix(s, tiles_per_seq, a_ref, halo_ref, u_ref, v_ref, ga_ref, gb_ref, pw_ref, ps_ref, sg_ref,
                sw_ref, sbt_ref, ya_ref, yb_ref, st_ref, f_ref, p_ref, q_ref):
    rows, width = a_ref.shape
    gdim = width // POOL_GROUPS
    hdim = width // SGU_HEADS
    end = MIX_FRONT + rows

    f_ref[0:MIX_FRONT - HALO, :] = jnp.zeros((MIX_FRONT - HALO, width), F32)
    f_ref[MIX_FRONT - HALO:MIX_FRONT, :] = jnp.where(s == 0, 0.0, halo_ref[...])
    f_ref[MIX_FRONT:, :] = a_ref[...]
    pos = s * rows + lax.broadcasted_iota(jnp.int32, (rows, 1), 0)
    for g, w in enumerate(POOL_WINDOWS):
        cs = slice(g * gdim, (g + 1) * gdim)
        tok = a_ref[:, cs]
        stages = w.bit_length() - 1
        assert 1 << stages == w and 8 * (stages - 1) + w // 2 <= MIX_FRONT
        src, cols = f_ref, cs
        for j in range(1, stages + 1):
            shift = 1 << (j - 1)
            start = MIX_FRONT - 8 * (stages - j)
            acc = src[start:end, cols] + src[start - shift:end - shift, cols]
            if j < stages:
                dst = (p_ref, q_ref)[j % 2]
                dst[start:end, :] = acc
                src, cols = dst, slice(0, gdim)
        cnt = jnp.minimum(pos + 1, w).astype(F32)
        pooled = (acc / cnt - tok).astype(BF16)
        mixed = jnp.dot(pooled, pw_ref[g], preferred_element_type=F32)
        ya_ref[:, cs] = (mixed * ps_ref[:, cs] * ga_ref[:, cs].astype(F32)).astype(BF16)

    vn = _rms(v_ref[...], sg_ref[...]).astype(BF16)
    for h in range(SGU_HEADS):
        cs = slice(h * hdim, (h + 1) * hdim)
        wsh = _tril_bf16(sw_ref[h])
        bias = sbt_ref[:, h:h + 1]
        for c in range(rows // CHUNK):
            rs = slice(c * CHUNK, (c + 1) * CHUNK)
            z = jnp.dot(wsh, vn[rs, cs], preferred_element_type=F32) + bias
            yb_ref[rs, cs] = (u_ref[rs, cs] * z * gb_ref[rs, cs].astype(F32)).astype(BF16)

    @pl.when(s == tiles_per_seq - 1)
    def _():
        st_ref[...] = f_ref[end - POOL_BUF:end, :]


def _prompt_mix_specs(auv, gates, pw, ps, sg, sw, sbt, layer, batch, tiles_per_seq, tile_of):
    m, width3 = auv.shape
    width = width3 // 3
    halo_per_tile = MIX_ROWS // HALO
    once = pl.Buffered(1)
    rows = lambda col: pl.BlockSpec((MIX_ROWS, width), lambda *g: (tile_of(*g), col))
    in_specs = [
        rows(0),
        pl.BlockSpec((HALO, width), lambda *g: (jnp.maximum(tile_of(*g) * halo_per_tile - 1, 0), 0)),
        rows(1), rows(2), rows(0), rows(1),
        pl.BlockSpec((None,) + pw.shape[1:], lambda *g: (layer, 0, 0, 0), pipeline_mode=once),
        pl.BlockSpec((None, 1, width), lambda *g: (layer, 0, 0)),
        pl.BlockSpec((None, 1, width), lambda *g: (layer, 0, 0)),
        pl.BlockSpec((None,) + sw.shape[1:], lambda *g: (layer, 0, 0, 0), pipeline_mode=once),
        pl.BlockSpec((None,) + sbt.shape[1:], lambda *g: (layer, 0, 0)),
    ]
    assert len(in_specs) == N_MIX_IN
    args = [auv, auv, auv, auv, gates, gates, pw, ps, sg, sw, sbt]
    out_specs = [rows(0), rows(0),
                 pl.BlockSpec((None, POOL_BUF, width), lambda *g: (tile_of(*g) // tiles_per_seq, 0, 0))]
    out_shape = [jax.ShapeDtypeStruct((m, width), BF16),
                 jax.ShapeDtypeStruct((m, width), BF16),
                 jax.ShapeDtypeStruct((batch, POOL_BUF, width), F32)]
    scratch = [pltpu.VMEM((MIX_FRONT + MIX_ROWS, width), F32),
               pltpu.VMEM((MIX_FRONT + MIX_ROWS, width // POOL_GROUPS), F32),
               pltpu.VMEM((MIX_FRONT + MIX_ROWS, width // POOL_GROUPS), F32)]
    return in_specs, args, out_specs, out_shape, scratch


def _mix_sample_kernel(n_aliased, a_ref, u_ref, v_ref, ga_ref, gb_ref, buf_ref, pw_ref, ps_ref, sg_ref,
                       sw_ref, *refs):
    ya_ref, yb_ref, nb_ref, vn_ref = refs[n_aliased:]
    rows, width = a_ref.shape
    nseq = nb_ref.shape[1]
    steps = rows // nseq
    gdim = width // POOL_GROUPS
    slab = lambda t: slice(t * nseq, (t + 1) * nseq)

    def full(i, cs):
        if i < POOL_BUF:
            return buf_ref[i, :, cs]
        return a_ref[slab(i - POOL_BUF), cs]

    for r in range(POOL_BUF):
        nb_ref[r] = full(steps + r, slice(0, width))

    for t in range(steps):
        for g, w in enumerate(POOL_WINDOWS):
            cs = slice(g * gdim, (g + 1) * gdim)
            tok = full(POOL_BUF + t, cs)
            acc = tok
            for k in range(1, w):
                acc = acc + full(POOL_BUF + t - k, cs)
            cnt = float(min(PAST_LEN + t + 1, w))
            pooled = (acc / cnt - tok).astype(BF16)
            mixed = jnp.dot(pooled, pw_ref[g], preferred_element_type=F32)
            ya_ref[slab(t), cs] = (mixed * ps_ref[:, cs] * ga_ref[slab(t), cs].astype(F32)).astype(BF16)

    vn = [_rms(v_ref[slab(t), :], sg_ref[...]) for t in range(steps)]
    for t in range(steps):
        vn_ref[:, t, :] = vn[t]
    vnb = [x.astype(BF16).astype(F32) for x in vn]
    for t in range(steps):
        z = sw_ref[steps, t:t + 1, :]
        for s in range(t + 1):
            z = z + sw_ref[s, t:t + 1, :] * vnb[s]
        yb_ref[slab(t), :] = (u_ref[slab(t), :] * z * gb_ref[slab(t), :].astype(F32)).astype(BF16)


def _mix_sample(auv, gates, ya, yb, state_pool, carried, pw, ps, sg, swx, layer, m_prompt, steps):
    m, width3 = auv.shape
    width = width3 // 3
    depth, _, nseq, _ = state_pool.shape
    rows = MIX_SEQS * steps
    first = m_prompt // rows
    row = lambda i: first + i
    any_spec = pl.BlockSpec(memory_space=pl.ANY)
    aliased = (ya, yb) + (tuple(carried) if carried is not None else ())
    n_in = 10
    return pl.pallas_call(
        functools.partial(_mix_sample_kernel, len(aliased)),
        grid=(nseq // MIX_SEQS,),
        in_specs=[
            pl.BlockSpec((rows, width), lambda i: (row(i), 0)),
            pl.BlockSpec((rows, width), lambda i: (row(i), 1)),
            pl.BlockSpec((rows, width), lambda i: (row(i), 2)),
            pl.BlockSpec((rows, width), lambda i: (row(i), 0)),
            pl.BlockSpec((rows, width), lambda i: (row(i), 1)),
            pl.BlockSpec((None, POOL_BUF, MIX_SEQS, width), lambda i: (layer, 0, i, 0)),
            pl.BlockSpec((None,) + pw.shape[1:], lambda i: (layer, 0, 0, 0)),
            pl.BlockSpec((None, 1, width), lambda i: (layer, 0, 0)),
            pl.BlockSpec((None, 1, width), lambda i: (layer, 0, 0)),
            pl.BlockSpec((None,) + swx.shape[1:], lambda i: (layer, 0, 0, 0)),
        ] + [any_spec] * len(aliased),
        out_specs=[
            pl.BlockSpec((rows, width), lambda i: (row(i), 0)),
            pl.BlockSpec((rows, width), lambda i: (row(i), 0)),
            pl.BlockSpec((None, POOL_BUF, MIX_SEQS, width), lambda i: (layer, 0, i, 0)),
            pl.BlockSpec((None, MIX_SEQS, steps, width), lambda i: (layer, i, 0, 0)),
        ],
        out_shape=[jax.ShapeDtypeStruct(ya.shape, ya.dtype),
                   jax.ShapeDtypeStruct(yb.shape, yb.dtype),
                   jax.ShapeDtypeStruct((depth, POOL_BUF, nseq, width), F32),
                   jax.ShapeDtypeStruct((depth, nseq, steps, width), F32)],
        input_output_aliases={n_in + k: k for k in range(len(aliased))},
        compiler_params=_params("arbitrary"),
        name="mix_sample",
    )(auv, auv, auv, gates, gates, state_pool, pw, ps, sg, swx, *aliased)


def _merge_kernel(layer, ya_ref, yb_ref, wa_hbm, wb_hbm, sa_ref, sb_ref, o_ref, *scratch):
    k = ya_ref.shape[1]
    per = k // W_CHUNK

    def chunk_src(c, jt):
        if c < per:
            return _col_window(wa_hbm, layer, c, jt)
        return _col_window(wb_hbm, layer, c - per, jt)

    def body(w_ref):
        for q in range(COL_TILE // DOT_COLS):
            cs = slice(q * DOT_COLS, (q + 1) * DOT_COLS)
            pa = jnp.dot(ya_ref[...], w_ref[0:k, cs], preferred_element_type=F32)
            pb = jnp.dot(yb_ref[...], w_ref[k:2 * k, cs], preferred_element_type=F32)
            o_ref[:, cs] = (sa_ref[:, cs].astype(F32) * pa + sb_ref[:, cs].astype(F32) * pb).astype(o_ref.dtype)

    _weight_pipeline(chunk_src, 2 * per, *scratch, body)


def _merge(ya, yb, wa, wb, sel, layer):
    m, k = ya.shape
    n = wa.shape[-1]
    nb = n // COL_TILE
    assert m // ROW_TILE >= 2 * k // W_CHUNK + 1
    any_spec = pl.BlockSpec(memory_space=pl.ANY)
    return pl.pallas_call(
        functools.partial(_merge_kernel, layer),
        grid=(nb, m // ROW_TILE),
        in_specs=[pl.BlockSpec((ROW_TILE, k), lambda j, i: (i, 0)),
                  pl.BlockSpec((ROW_TILE, k), lambda j, i: (i, 0)),
                  any_spec, any_spec,
                  pl.BlockSpec((ROW_TILE, COL_TILE), lambda j, i: (i, j)),
                  pl.BlockSpec((ROW_TILE, COL_TILE), lambda j, i: (i, j + nb))],
        out_specs=pl.BlockSpec((ROW_TILE, COL_TILE), lambda j, i: (i, j)),
        out_shape=jax.ShapeDtypeStruct((m, n), BF16),
        scratch_shapes=_weight_scratch(2 * k),
        compiler_params=_params("arbitrary", "arbitrary"),
        name="branch_merge",
    )(ya, yb, wa, wb, sel, sel)


def _out_proj_kernel(layer, feeds_next, mg_ref, w_hbm, x_ref, *refs):
    if feeds_next:
        g_ref, o_ref, xb_ref, ssq_ref = refs[:4]
    else:
        o_ref = refs[0]
    k = mg_ref.shape[1]

    def body(w_ref):
        parts = []
        for q in range(COL_TILE // DOT_COLS):
            cs = slice(q * DOT_COLS, (q + 1) * DOT_COLS)
            y = x_ref[:, cs] + jnp.dot(mg_ref[...], w_ref[:, cs], preferred_element_type=F32)
            o_ref[:, cs] = y
            if feeds_next:
                xb_ref[:, cs] = (y * g_ref[:, cs]).astype(BF16)
                parts += _lane_partials(y * y, 1)
        if feeds_next:
            ssq_ref[...] = functools.reduce(lambda a, b: a + b, parts)

    _weight_pipeline(lambda c, jt: _col_window(w_hbm, layer, c, jt), k // W_CHUNK, *refs[-4:], body)


def _out_proj(merged, w_out, x, layer, g_all):
    m, k = merged.shape
    n = w_out.shape[-1]
    assert m // ROW_TILE >= k // W_CHUNK + 1
    feeds_next = g_all is not None
    tile = pl.BlockSpec((ROW_TILE, COL_TILE), lambda j, i: (i, j))
    in_specs = [pl.BlockSpec((ROW_TILE, k), lambda j, i: (i, 0)), pl.BlockSpec(memory_space=pl.ANY), tile]
    args = [merged, w_out, x]
    out_specs, out_shape = [tile], [jax.ShapeDtypeStruct((m, n), F32)]
    if feeds_next:
        assert n // COL_TILE == SSQ_GROUPS
        in_specs.append(pl.BlockSpec((None, 1, COL_TILE), lambda j, i: (layer + 1, 0, j)))
        args.append(g_all)
        out_specs += [tile, pl.BlockSpec((ROW_TILE, LANES), lambda j, i: (i, j))]
        out_shape += [jax.ShapeDtypeStruct((m, n), BF16), jax.ShapeDtypeStruct((m, SSQ_GROUPS * LANES), F32)]
    return pl.pallas_call(
        functools.partial(_out_proj_kernel, layer, feeds_next),
        grid=(n // COL_TILE, m // ROW_TILE),
        in_specs=in_specs,
        out_specs=out_specs,
        out_shape=out_shape,
        scratch_shapes=_weight_scratch(k),
        compiler_params=_params("arbitrary", "arbitrary"),
        name="out_proj",
    )(*args)


def kernel(x_prompt, x_sample, state_pool, norm_g, w_in, pool_w, pool_scale, sgu_norm_g, sgu_w, sgu_b,
           w_branch_a, w_branch_b, w_out, final_norm_g):
    batch, seq, d = x_prompt.shape
    nseq, steps, _ = x_sample.shape
    depth = w_in.shape[0]
    width = pool_scale.shape[-1]
    hdim = width // SGU_HEADS
    m_prompt = batch * seq
    assert PAST_LEN % CHUNK == 0 and steps <= CHUNK
    wt = width // COL_TILE
    ident_cols = lambda j: jnp.where(j < wt, j, j + wt)
    silu_cols = lambda j: jnp.where(j < wt, j + wt, j + 3 * wt)
    sigm_cols = lambda j: j + 5 * wt

    seq_tiles = nseq // MIX_SEQS
    xs = jnp.swapaxes(x_sample.reshape(seq_tiles, MIX_SEQS, steps, d), 1, 2).reshape(nseq * steps, d)
    g_all = norm_g.reshape(depth, 1, d)
    x, xb, ssq = _prep(x_prompt.reshape(m_prompt, d), xs, g_all[0])
    pw = pool_w.astype(BF16)
    ps = pool_scale.reshape(depth, 1, width)
    sg = sgu_norm_g.reshape(depth, 1, width)
    sbt = jnp.swapaxes(sgu_b, 1, 2)
    sw_small = sgu_w[:, :, :steps, :steps].astype(BF16).astype(F32)
    swx = jnp.repeat(jnp.transpose(sw_small, (0, 3, 2, 1)), hdim, axis=-1)
    bx = jnp.repeat(jnp.swapaxes(sgu_b[:, :, :steps], 1, 2), hdim, axis=-1)[:, None]
    swx = jnp.concatenate([swx, bx], axis=1)

    pool_t = jnp.swapaxes(state_pool, 1, 2)
    pool_p, carried = [], None
    for l in range(depth):
        auv, r = _proj(xb, ssq, w_in, l, ident_cols, 3 * wt, "none", F32, emits_r=True)
        gates, = _proj(xb, r, w_in, l, silu_cols, 2 * wt, "silu", BF16)
        sel, ya, yb, st_p = _proj(xb, r, w_in, l, sigm_cols, 2 * (d // COL_TILE), "sigmoid", BF16,
                                  mix_args=(auv, gates, pw, ps, sg, sgu_w, sbt, batch, seq))
        ya, yb, *carried = _mix_sample(auv, gates, ya, yb, pool_t, carried, pw, ps, sg, swx, l,
                                       m_prompt, steps)
        merged = _merge(ya, yb, w_branch_a, w_branch_b, sel, l)
        if l + 1 < depth:
            x, xb, ssq = _out_proj(merged, w_out, x, l, g_all)
        else:
            x, = _out_proj(merged, w_out, x, l, None)
        pool_p.append(st_p)

    y_p, y_s = _final_norm(x, final_norm_g.reshape(1, d), m_prompt)
    y_s = jnp.swapaxes(y_s.reshape(seq_tiles, steps, MIX_SEQS, d), 1, 2)
    return (y_p.reshape(batch, seq, d), y_s.reshape(nseq, steps, d),
            jnp.stack(pool_p, axis=0), jnp.swapaxes(carried[0], 1, 2), carried[1])
```

```python
import functools

import jax
import jax.numpy as jnp
from jax import lax
from jax.experimental import pallas as pl
from jax.experimental.pallas import tpu as pltpu

F32 = jnp.float32
BF16 = jnp.bfloat16

POOL_WINDOWS = (2, 4, 8, 16)
POOL_GROUPS = len(POOL_WINDOWS)
POOL_BUF = max(POOL_WINDOWS) - 1
HALO = POOL_BUF + 1
SGU_HEADS = 8
CHUNK = 128
EPS = 1e-6
PAST_LEN = 16384

V7X_VMEM_LIMIT_BYTES = 60 * 1024 * 1024

LANES = 128
ROW_TILE = 1024
COL_TILE = 1024
DOT_COLS = 256
GATE_DOT_COLS = 256
W_CHUNK = 512
SSQ_GROUPS = 4
PREP_ROWS = 256
NORM_ROWS = 512
MIX_ROWS = 128
MIX_FRONT = 2 * HALO
MIX_SEQS = 32


def _params(*semantics):
    return pltpu.CompilerParams(dimension_semantics=semantics,
                                vmem_limit_bytes=V7X_VMEM_LIMIT_BYTES)


def _rms(x, g):
    ms = jnp.mean(x * x, axis=-1, keepdims=True)
    return x * lax.rsqrt(ms + EPS) * g


def _lane_partials(sq, groups):
    nblk = sq.shape[1] // LANES
    per = nblk // groups
    outs = []
    for q in range(groups):
        acc = sq[:, q * per * LANES:(q * per + 1) * LANES]
        for b in range(q * per + 1, (q + 1) * per):
            acc = acc + sq[:, b * LANES:(b + 1) * LANES]
        outs.append(acc)
    return outs


def _prep_kernel(n_prompt_tiles, xp_ref, xs_ref, g_ref, x_ref, xb_ref, ssq_ref):
    i = pl.program_id(0)

    def emit(src_ref):
        x = src_ref[...]
        x_ref[...] = x
        xb_ref[...] = (x * g_ref[...]).astype(BF16)
        for q, part in enumerate(_lane_partials(x * x, SSQ_GROUPS)):
            ssq_ref[:, q * LANES:(q + 1) * LANES] = part

    @pl.when(i < n_prompt_tiles)
    def _():
        emit(xp_ref)

    @pl.when(i >= n_prompt_tiles)
    def _():
        emit(xs_ref)


def _prep(xp, xs, g):
    mp, d = xp.shape
    m = mp + xs.shape[0]
    npt = mp // PREP_ROWS
    return pl.pallas_call(
        functools.partial(_prep_kernel, npt),
        grid=(m // PREP_ROWS,),
        in_specs=[pl.BlockSpec((PREP_ROWS, d), lambda i: (jnp.minimum(i, npt - 1), 0)),
                  pl.BlockSpec((PREP_ROWS, d), lambda i: (jnp.maximum(i - npt, 0), 0)),
                  pl.BlockSpec((1, d), lambda i: (0, 0))],
        out_specs=[pl.BlockSpec((PREP_ROWS, d), lambda i: (i, 0)),
                   pl.BlockSpec((PREP_ROWS, d), lambda i: (i, 0)),
                   pl.BlockSpec((PREP_ROWS, SSQ_GROUPS * LANES), lambda i: (i, 0))],
        out_shape=[jax.ShapeDtypeStruct((m, d), F32),
                   jax.ShapeDtypeStruct((m, d), BF16),
                   jax.ShapeDtypeStruct((m, SSQ_GROUPS * LANES), F32)],
        compiler_params=_params("arbitrary"),
        name="prep",
    )(xp, xs, g)


def _final_norm_kernel(n_prompt_tiles, x_ref, g_ref, op_ref, os_ref):
    y = _rms(x_ref[...], g_ref[...])
    i = pl.program_id(0)

    @pl.when(i < n_prompt_tiles)
    def _():
        op_ref[...] = y

    @pl.when(i >= n_prompt_tiles)
    def _():
        os_ref[...] = y


def _final_norm(x, g, m_prompt):
    m, d = x.shape
    npt = m_prompt // NORM_ROWS
    return pl.pallas_call(
        functools.partial(_final_norm_kernel, npt),
        grid=(m // NORM_ROWS,),
        in_specs=[pl.BlockSpec((NORM_ROWS, d), lambda i: (i, 0)),
                  pl.BlockSpec((1, d), lambda i: (0, 0))],
        out_specs=[pl.BlockSpec((NORM_ROWS, d), lambda i: (jnp.minimum(i, npt - 1), 0)),
                   pl.BlockSpec((NORM_ROWS, d), lambda i: (jnp.maximum(i - npt, 0), 0))],
        out_shape=[jax.ShapeDtypeStruct((m_prompt, d), F32),
                   jax.ShapeDtypeStruct((m - m_prompt, d), F32)],
        compiler_params=_params("arbitrary"),
        name="final_norm",
    )(x, g)


def _weight_scratch(k):
    return [pltpu.VMEM((k, COL_TILE), BF16),
            pltpu.VMEM((k, COL_TILE), BF16),
            pltpu.VMEM((2, W_CHUNK, COL_TILE), F32),
            pltpu.SemaphoreType.DMA((2,))]


def _weight_pipeline(chunk_src, n_chunks, w_even_ref, w_odd_ref, ring_ref, sem_ref, body):
    j, i = pl.program_id(0), pl.program_id(1)

    def copy(c, jt):
        return pltpu.make_async_copy(chunk_src(c, jt), ring_ref.at[c % 2], sem_ref.at[c % 2])

    @pl.when((j == 0) & (i == 0))
    def _():
        copy(0, 0).start()
        for c in range(n_chunks):
            if c + 1 < n_chunks:
                copy(c + 1, 0).start()
            copy(c, 0).wait()
            w_even_ref[c * W_CHUNK:(c + 1) * W_CHUNK, :] = ring_ref[c % 2].astype(BF16)

    has_next = j + 1 < pl.num_programs(0)
    for c in range(n_chunks):
        @pl.when(has_next & (i == c + 1))
        def _():
            copy(c, j + 1).wait()

        @pl.when(has_next & (i == c))
        def _():
            copy(c, j + 1).start()

    def step(w_ref, next_ref):
        c = jnp.clip(i - 1, 0, n_chunks - 1)
        row0 = pl.multiple_of(c * W_CHUNK, W_CHUNK)
        body(w_ref)
        next_ref[pl.ds(row0, W_CHUNK), :] = ring_ref[(i + 1) % 2].astype(BF16)

    @pl.when(j % 2 == 0)
    def _():
        step(w_even_ref, w_odd_ref)

    @pl.when(j % 2 == 1)
    def _():
        step(w_odd_ref, w_even_ref)


def _col_window(w_hbm, layer, c, col_block):
    col0 = col_block * COL_TILE
    if not isinstance(col0, int):
        col0 = pl.multiple_of(col0, COL_TILE)
    return w_hbm.at[layer, pl.ds(c * W_CHUNK, W_CHUNK), pl.ds(col0, COL_TILE)]


N_MIX_IN = 11


def _proj_kernel(act, layer, col_block_of, emits_r, mix, xb_ref, stat_ref, w_hbm, *refs):
    refs = list(refs)
    mix_in = [refs.pop(0) for _ in range(N_MIX_IN)] if mix else []
    o_ref = refs.pop(0)
    r_ref = refs.pop(0) if emits_r else None
    mix_out = [refs.pop(0) for _ in range(3)] if mix else []
    scratch = refs
    k = xb_ref.shape[1]
    step = pl.program_id(0) * pl.num_programs(1) + pl.program_id(1)

    def row_scale():
        r = lax.rsqrt(jnp.sum(stat_ref[...], axis=-1, keepdims=True) / k + EPS)
        return jnp.broadcast_to(r, (r.shape[0], LANES))

    if emits_r:
        @pl.when(pl.program_id(0) == 0)
        def _():
            r_ref[...] = row_scale()

    if mix:
        tiles_per_seq, n_tiles = mix

        @pl.when(step < n_tiles)
        def _():
            _prompt_mix(step % tiles_per_seq, tiles_per_seq, *mix_in, *mix_out, *scratch[4:])

    def body(w_ref):
        r_lanes = row_scale() if emits_r else stat_ref[...]
        dot_cols = DOT_COLS if act == "none" else GATE_DOT_COLS
        r_cols = jnp.concatenate([r_lanes] * (dot_cols // LANES), axis=1)
        for q in range(COL_TILE // dot_cols):
            cs = slice(q * dot_cols, (q + 1) * dot_cols)
            p = jnp.dot(xb_ref[...], w_ref[:, cs], preferred_element_type=F32) * r_cols
            if act != "none":
                gate = 0.5 * jnp.tanh(0.5 * p) + 0.5
                p = p * gate if act == "silu" else gate
            o_ref[:, cs] = p.astype(o_ref.dtype)

    _weight_pipeline(lambda c, jt: _col_window(w_hbm, layer, c, col_block_of(jt)),
                     k // W_CHUNK, *scratch[:4], body)


def _proj(xb, stat, w_in, layer, col_block_of, n_col_blocks, act, out_dtype, emits_r=False, mix_args=None):
    m, k = xb.shape
    n_rows = m // ROW_TILE
    assert n_rows >= k // W_CHUNK + 1
    tile = pl.BlockSpec((ROW_TILE, COL_TILE), lambda j, i: (i, j))
    in_specs = [pl.BlockSpec((ROW_TILE, k), lambda j, i: (i, 0)),
                pl.BlockSpec((ROW_TILE, stat.shape[1]), lambda j, i: (i, 0)),
                pl.BlockSpec(memory_space=pl.ANY)]
    args = [xb, stat, w_in]
    out_specs, out_shape = [tile], [jax.ShapeDtypeStruct((m, n_col_blocks * COL_TILE), out_dtype)]
    scratch = _weight_scratch(k)
    if emits_r:
        out_specs.append(pl.BlockSpec((ROW_TILE, LANES), lambda j, i: (jnp.where(j == 0, i, n_rows - 1), 0)))
        out_shape.append(jax.ShapeDtypeStruct((m, LANES), F32))
    mix = None
    if mix_args is not None:
        auv, gates, pw, ps, sg, sw, sbt, batch, seq = mix_args
        tiles_per_seq = seq // MIX_ROWS
        mix = (tiles_per_seq, batch * tiles_per_seq)
        assert n_col_blocks * n_rows >= mix[1]
        tile_of = lambda j, i: jnp.minimum(j * n_rows + i, mix[1] - 1)
        mi, margs, mo, mshape, mscratch = _prompt_mix_specs(auv, gates, pw, ps, sg, sw, sbt, layer, batch,
                                                             tiles_per_seq, tile_of)
        in_specs += mi
        args += margs
        out_specs += mo
        out_shape += mshape
        scratch = scratch + mscratch
    return pl.pallas_call(
        functools.partial(_proj_kernel, act, layer, col_block_of, emits_r, mix),
        grid=(n_col_blocks, n_rows),
        in_specs=in_specs,
        out_specs=out_specs,
        out_shape=out_shape,
        scratch_shapes=scratch,
        compiler_params=_params("arbitrary", "arbitrary"),
        name="in_proj_" + act,
    )(*args)


def _tril_bf16(w):
    n = w.shape[-1]
    row = lax.broadcasted_iota(jnp.int32, (n, n), 0)
    col = lax.broadcasted_iota(jnp.int32, (n, n), 1)
    return jnp.where(col <= row, w, 0.0).astype(BF16)


def _prompt_mix(s, tiles_per_seq, a_ref, halo_ref, u_ref, v_ref, ga_ref, gb_ref, pw_ref, ps_ref, sg_ref,
                sw_ref, sbt_ref, ya_ref, yb_ref, st_ref, f_ref, p_ref, q_ref):
    rows, width = a_ref.shape
    gdim = width // POOL_GROUPS
    hdim = width // SGU_HEADS
    end = MIX_FRONT + rows

    f_ref[0:MIX_FRONT - HALO, :] = jnp.zeros((MIX_FRONT - HALO, width), F32)
    f_ref[MIX_FRONT - HALO:MIX_FRONT, :] = jnp.where(s == 0, 0.0, halo_ref[...])
    f_ref[MIX_FRONT:, :] = a_ref[...]
    pos = s * rows + lax.broadcasted_iota(jnp.int32, (rows, 1), 0)
    for g, w in enumerate(POOL_WINDOWS):
        cs = slice(g * gdim, (g + 1) * gdim)
        tok = a_ref[:, cs]
        stages = w.bit_length() - 1
        assert 1 << stages == w and 8 * (stages - 1) + w // 2 <= MIX_FRONT
        src, cols = f_ref, cs
        for j in range(1, stages + 1):
            shift = 1 << (j - 1)
            start = MIX_FRONT - 8 * (stages - j)
            acc = src[start:end, cols] + src[start - shift:end - shift, cols]
            if j < stages:
                dst = (p_ref, q_ref)[j % 2]
                dst[start:end, :] = acc
                src, cols = dst, slice(0, gdim)
        cnt = jnp.minimum(pos + 1, w).astype(F32)
        pooled = (acc / cnt - tok).astype(BF16)
        mixed = jnp.dot(pooled, pw_ref[g], preferred_element_type=F32)
        ya_ref[:, cs] = (mixed * ps_ref[:, cs] * ga_ref[:, cs].astype(F32)).astype(BF16)

    vn = _rms(v_ref[...], sg_ref[...]).astype(BF16)
    for h in range(SGU_HEADS):
        cs = slice(h * hdim, (h + 1) * hdim)
        wsh = _tril_bf16(sw_ref[h])
        bias = sbt_ref[:, h:h + 1]
        for c in range(rows // CHUNK):
            rs = slice(c * CHUNK, (c + 1) * CHUNK)
            z = jnp.dot(wsh, vn[rs, cs], preferred_element_type=F32) + bias
            yb_ref[rs, cs] = (u_ref[rs, cs] * z * gb_ref[rs, cs].astype(F32)).astype(BF16)

    @pl.when(s == tiles_per_seq - 1)
    def _():
        st_ref[...] = f_ref[end - POOL_BUF:end, :]


def _prompt_mix_specs(auv, gates, pw, ps, sg, sw, sbt, layer, batch, tiles_per_seq, tile_of):
    m, width3 = auv.shape
    width = width3 // 3
    halo_per_tile = MIX_ROWS // HALO
    once = pl.Buffered(1)
    rows = lambda col: pl.BlockSpec((MIX_ROWS, width), lambda *g: (tile_of(*g), col))
    in_specs = [
        rows(0),
        pl.BlockSpec((HALO, width), lambda *g: (jnp.maximum(tile_of(*g) * halo_per_tile - 1, 0), 0)),
        rows(1), rows(2), rows(0), rows(1),
        pl.BlockSpec((None,) + pw.shape[1:], lambda *g: (layer, 0, 0, 0), pipeline_mode=once),
        pl.BlockSpec((None, 1, width), lambda *g: (layer, 0, 0)),
        pl.BlockSpec((None, 1, width), lambda *g: (layer, 0, 0)),
        pl.BlockSpec((None,) + sw.shape[1:], lambda *g: (layer, 0, 0, 0), pipeline_mode=once),
        pl.BlockSpec((None,) + sbt.shape[1:], lambda *g: (layer, 0, 0)),
    ]
    assert len(in_specs) == N_MIX_IN
    args = [auv, auv, auv, auv, gates, gates, pw, ps, sg, sw, sbt]
    out_specs = [rows(0), rows(0),
                 pl.BlockSpec((None, POOL_BUF, width), lambda *g: (tile_of(*g) // tiles_per_seq, 0, 0))]
    out_shape = [jax.ShapeDtypeStruct((m, width), BF16),
                 jax.ShapeDtypeStruct((m, width), BF16),
                 jax.ShapeDtypeStruct((batch, POOL_BUF, width), F32)]
    scratch = [pltpu.VMEM((MIX_FRONT + MIX_ROWS, width), F32),
               pltpu.VMEM((MIX_FRONT + MIX_ROWS, width // POOL_GROUPS), F32),
               pltpu.VMEM((MIX_FRONT + MIX_ROWS, width // POOL_GROUPS), F32)]
    return in_specs, args, out_specs, out_shape, scratch


def _mix_sample_kernel(n_aliased, a_ref, u_ref, v_ref, ga_ref, gb_ref, buf_ref, pw_ref, ps_ref, sg_ref,
                       sw_ref, *refs):
    ya_ref, yb_ref, nb_ref, vn_ref = refs[n_aliased:]
    rows, width = a_ref.shape
    nseq = nb_ref.shape[1]
    steps = rows // nseq
    gdim = width // POOL_GROUPS
    slab = lambda t: slice(t * nseq, (t + 1) * nseq)

    def full(i, cs):
        if i < POOL_BUF:
            return buf_ref[i, :, cs]
        return a_ref[slab(i - POOL_BUF), cs]

    for r in range(POOL_BUF):
        nb_ref[r] = full(steps + r, slice(0, width))

    for t in range(steps):
        for g, w in enumerate(POOL_WINDOWS):
            cs = slice(g * gdim, (g + 1) * gdim)
            tok = full(POOL_BUF + t, cs)
            acc = tok
            for k in range(1, w):
                acc = acc + full(POOL_BUF + t - k, cs)
            cnt = float(min(PAST_LEN + t + 1, w))
            pooled = (acc / cnt - tok).astype(BF16)
            mixed = jnp.dot(pooled, pw_ref[g], preferred_element_type=F32)
            ya_ref[slab(t), cs] = (mixed * ps_ref[:, cs] * ga_ref[slab(t), cs].astype(F32)).astype(BF16)

    vn = [_rms(v_ref[slab(t), :], sg_ref[...]) for t in range(steps)]
    for t in range(steps):
        vn_ref[:, t, :] = vn[t]
    vnb = [x.astype(BF16).astype(F32) for x in vn]
    for t in range(steps):
        z = sw_ref[steps, t:t + 1, :]
        for s in range(t + 1):
            z = z + sw_ref[s, t:t + 1, :] * vnb[s]
        yb_ref[slab(t), :] = (u_ref[slab(t), :] * z * gb_ref[slab(t), :].astype(F32)).astype(BF16)


def _mix_sample(auv, gates, ya, yb, state_pool, carried, pw, ps, sg, swx, layer, m_prompt, steps):
    m, width3 = auv.shape
    width = width3 // 3
    depth, _, nseq, _ = state_pool.shape
    rows = MIX_SEQS * steps
    first = m_prompt // rows
    row = lambda i: first + i
    any_spec = pl.BlockSpec(memory_space=pl.ANY)
    aliased = (ya, yb) + (tuple(carried) if carried is not None else ())
    n_in = 10
    return pl.pallas_call(
        functools.partial(_mix_sample_kernel, len(aliased)),
        grid=(nseq // MIX_SEQS,),
        in_specs=[
            pl.BlockSpec((rows, width), lambda i: (row(i), 0)),
            pl.BlockSpec((rows, width), lambda i: (row(i), 1)),
            pl.BlockSpec((rows, width), lambda i: (row(i), 2)),
            pl.BlockSpec((rows, width), lambda i: (row(i), 0)),
            pl.BlockSpec((rows, width), lambda i: (row(i), 1)),
            pl.BlockSpec((None, POOL_BUF, MIX_SEQS, width), lambda i: (layer, 0, i, 0)),
            pl.BlockSpec((None,) + pw.shape[1:], lambda i: (layer, 0, 0, 0)),
            pl.BlockSpec((None, 1, width), lambda i: (layer, 0, 0)),
            pl.BlockSpec((None, 1, width), lambda i: (layer, 0, 0)),
            pl.BlockSpec((None,) + swx.shape[1:], lambda i: (layer, 0, 0, 0)),
        ] + [any_spec] * len(aliased),
        out_specs=[
            pl.BlockSpec((rows, width), lambda i: (row(i), 0)),
            pl.BlockSpec((rows, width), lambda i: (row(i), 0)),
            pl.BlockSpec((None, POOL_BUF, MIX_SEQS, width), lambda i: (layer, 0, i, 0)),
            pl.BlockSpec((None, MIX_SEQS, steps, width), lambda i: (layer, i, 0, 0)),
        ],
        out_shape=[jax.ShapeDtypeStruct(ya.shape, ya.dtype),
                   jax.ShapeDtypeStruct(yb.shape, yb.dtype),
                   jax.ShapeDtypeStruct((depth, POOL_BUF, nseq, width), F32),
                   jax.ShapeDtypeStruct((depth, nseq, steps, width), F32)],
        input_output_aliases={n_in + k: k for k in range(len(aliased))},
        compiler_params=_params("arbitrary"),
        name="mix_sample",
    )(auv, auv, auv, gates, gates, state_pool, pw, ps, sg, swx, *aliased)


def _merge_kernel(layer, ya_ref, yb_ref, wa_hbm, wb_hbm, sa_ref, sb_ref, o_ref, *scratch):
    k = ya_ref.shape[1]
    per = k // W_CHUNK

    def chunk_src(c, jt):
        if c < per:
            return _col_window(wa_hbm, layer, c, jt)
        return _col_window(wb_hbm, layer, c - per, jt)

    def body(w_ref):
        for q in range(COL_TILE // DOT_COLS):
            cs = slice(q * DOT_COLS, (q + 1) * DOT_COLS)
            pa = jnp.dot(ya_ref[...], w_ref[0:k, cs], preferred_element_type=F32)
            pb = jnp.dot(yb_ref[...], w_ref[k:2 * k, cs], preferred_element_type=F32)
            o_ref[:, cs] = (sa_ref[:, cs].astype(F32) * pa + sb_ref[:, cs].astype(F32) * pb).astype(o_ref.dtype)

    _weight_pipeline(chunk_src, 2 * per, *scratch, body)


def _merge(ya, yb, wa, wb, sel, layer):
    m, k = ya.shape
    n = wa.shape[-1]
    nb = n // COL_TILE
    assert m // ROW_TILE >= 2 * k // W_CHUNK + 1
    any_spec = pl.BlockSpec(memory_space=pl.ANY)
    return pl.pallas_call(
        functools.partial(_merge_kernel, layer),
        grid=(nb, m // ROW_TILE),
        in_specs=[pl.BlockSpec((ROW_TILE, k), lambda j, i: (i, 0)),
                  pl.BlockSpec((ROW_TILE, k), lambda j, i: (i, 0)),
                  any_spec, any_spec,
                  pl.BlockSpec((ROW_TILE, COL_TILE), lambda j, i: (i, j)),
                  pl.BlockSpec((ROW_TILE, COL_TILE), lambda j, i: (i, j + nb))],
        out_specs=pl.BlockSpec((ROW_TILE, COL_TILE), lambda j, i: (i, j)),
        out_shape=jax.ShapeDtypeStruct((m, n), BF16),
        scratch_shapes=_weight_scratch(2 * k),
        compiler_params=_params("arbitrary", "arbitrary"),
        name="branch_merge",
    )(ya, yb, wa, wb, sel, sel)


def _out_proj_kernel(layer, feeds_next, mg_ref, w_hbm, x_ref, *refs):
    if feeds_next:
        g_ref, o_ref, xb_ref, ssq_ref = refs[:4]
    else:
        o_ref = refs[0]
    k = mg_ref.shape[1]

    def body(w_ref):
        parts = []
        for q in range(COL_TILE // DOT_COLS):
            cs = slice(q * DOT_COLS, (q + 1) * DOT_COLS)
            y = x_ref[:, cs] + jnp.dot(mg_ref[...], w_ref[:, cs], preferred_element_type=F32)
            o_ref[:, cs] = y
            if feeds_next:
                xb_ref[:, cs] = (y * g_ref[:, cs]).astype(BF16)
                parts += _lane_partials(y * y, 1)
        if feeds_next:
            ssq_ref[...] = functools.reduce(lambda a, b: a + b, parts)

    _weight_pipeline(lambda c, jt: _col_window(w_hbm, layer, c, jt), k // W_CHUNK, *refs[-4:], body)


def _out_proj(merged, w_out, x, layer, g_all):
    m, k = merged.shape
    n = w_out.shape[-1]
    assert m // ROW_TILE >= k // W_CHUNK + 1
    feeds_next = g_all is not None
    tile = pl.BlockSpec((ROW_TILE, COL_TILE), lambda j, i: (i, j))
    in_specs = [pl.BlockSpec((ROW_TILE, k), lambda j, i: (i, 0)), pl.BlockSpec(memory_space=pl.ANY), tile]
    args = [merged, w_out, x]
    out_specs, out_shape = [tile], [jax.ShapeDtypeStruct((m, n), F32)]
    if feeds_next:
        assert n // COL_TILE == SSQ_GROUPS
        in_specs.append(pl.BlockSpec((None, 1, COL_TILE), lambda j, i: (layer + 1, 0, j)))
        args.append(g_all)
        out_specs += [tile, pl.BlockSpec((ROW_TILE, LANES), lambda j, i: (i, j))]
        out_shape += [jax.ShapeDtypeStruct((m, n), BF16), jax.ShapeDtypeStruct((m, SSQ_GROUPS * LANES), F32)]
    return pl.pallas_call(
        functools.partial(_out_proj_kernel, layer, feeds_next),
        grid=(n // COL_TILE, m // ROW_TILE),
        in_specs=in_specs,
        out_specs=out_specs,
        out_shape=out_shape,
        scratch_shapes=_weight_scratch(k),
        compiler_params=_params("arbitrary", "arbitrary"),
        name="out_proj",
    )(*args)


def kernel(x_prompt, x_sample, state_pool, norm_g, w_in, pool_w, pool_scale, sgu_norm_g, sgu_w, sgu_b,
           w_branch_a, w_branch_b, w_out, final_norm_g):
    batch, seq, d = x_prompt.shape
    nseq, steps, _ = x_sample.shape
    depth = w_in.shape[0]
    width = pool_scale.shape[-1]
    hdim = width // SGU_HEADS
    m_prompt = batch * seq
    assert PAST_LEN % CHUNK == 0 and steps <= CHUNK
    wt = width // COL_TILE
    ident_cols = lambda j: jnp.where(j < wt, j, j + wt)
    silu_cols = lambda j: jnp.where(j < wt, j + wt, j + 3 * wt)
    sigm_cols = lambda j: j + 5 * wt

    seq_tiles = nseq // MIX_SEQS
    xs = jnp.swapaxes(x_sample.reshape(seq_tiles, MIX_SEQS, steps, d), 1, 2).reshape(nseq * steps, d)
    g_all = norm_g.reshape(depth, 1, d)
    x, xb, ssq = _prep(x_prompt.reshape(m_prompt, d), xs, g_all[0])
    pw = pool_w.astype(BF16)
    ps = pool_scale.reshape(depth, 1, width)
    sg = sgu_norm_g.reshape(depth, 1, width)
    sbt = jnp.swapaxes(sgu_b, 1, 2)
    sw_small = sgu_w[:, :, :steps, :steps].astype(BF16).astype(F32)
    swx = jnp.repeat(jnp.transpose(sw_small, (0, 3, 2, 1)), hdim, axis=-1)
    bx = jnp.repeat(jnp.swapaxes(sgu_b[:, :, :steps], 1, 2), hdim, axis=-1)[:, None]
    swx = jnp.concatenate([swx, bx], axis=1)

    pool_t = jnp.swapaxes(state_pool, 1, 2)
    pool_p, carried = [], None
    for l in range(depth):
        auv, r = _proj(xb, ssq, w_in, l, ident_cols, 3 * wt, "none", F32, emits_r=True)
        gates, = _proj(xb, r, w_in, l, silu_cols, 2 * wt, "silu", BF16)
        sel, ya, yb, st_p = _proj(xb, r, w_in, l, sigm_cols, 2 * (d // COL_TILE), "sigmoid", BF16,
                                  mix_args=(auv, gates, pw, ps, sg, sgu_w, sbt, batch, seq))
        ya, yb, *carried = _mix_sample(auv, gates, ya, yb, pool_t, carried, pw, ps, sg, swx, l,
                                       m_prompt, steps)
        merged = _merge(ya, yb, w_branch_a, w_branch_b, sel, l)
        if l + 1 < depth:
            x, xb, ssq = _out_proj(merged, w_out, x, l, g_all)
        else:
            x, = _out_proj(merged, w_out, x, l, None)
        pool_p.append(st_p)

    y_p, y_s = _final_norm(x, final_norm_g.reshape(1, d), m_prompt)
    y_s = jnp.swapaxes(y_s.reshape(seq_tiles, steps, MIX_SEQS, d), 1, 2)
    return (y_p.reshape(batch, seq, d), y_s.reshape(nseq, steps, d),
            jnp.stack(pool_p, axis=0), jnp.swapaxes(carried[0], 1, 2), carried[1])
```

```python
import functools

import jax
import jax.numpy as jnp
from jax import lax
from jax.experimental import pallas as pl
from jax.experimental.pallas import tpu as pltpu

F32 = jnp.float32
BF16 = jnp.bfloat16

POOL_WINDOWS = (2, 4, 8, 16)
POOL_GROUPS = len(POOL_WINDOWS)
POOL_BUF = max(POOL_WINDOWS) - 1
HALO = POOL_BUF + 1
SGU_HEADS = 8
CHUNK = 128
EPS = 1e-6
PAST_LEN = 16384

V7X_VMEM_LIMIT_BYTES = 60 * 1024 * 1024

LANES = 128
SUBLANES = 8
ROW_TILE = 1024
COL_TILE = 1024
DOT_COLS = 256
W_CHUNK = 512
SSQ_GROUPS = 4
PREP_ROWS = 256
NORM_ROWS = 512
MIX_ROWS = 128
MIX_FRONT = 2 * HALO
MIX_SEQS = 32


def _params(*semantics):
    return pltpu.CompilerParams(dimension_semantics=semantics,
                                vmem_limit_bytes=V7X_VMEM_LIMIT_BYTES)


def _rms(x, g):
    ms = jnp.mean(x * x, axis=-1, keepdims=True)
    return x * lax.rsqrt(ms + EPS) * g


def _lane_partials(sq, groups):
    nblk = sq.shape[1] // LANES
    per = nblk // groups
    outs = []
    for q in range(groups):
        acc = sq[:, q * per * LANES:(q * per + 1) * LANES]
        for b in range(q * per + 1, (q + 1) * per):
            acc = acc + sq[:, b * LANES:(b + 1) * LANES]
        outs.append(acc)
    return outs


def _prep_kernel(n_prompt_tiles, xp_ref, xs_ref, g_ref, x_ref, xb_ref, ssq_ref):
    i = pl.program_id(0)

    def emit(src_ref):
        x = src_ref[...]
        x_ref[...] = x
        xb_ref[...] = (x * g_ref[...]).astype(BF16)
        for q, part in enumerate(_lane_partials(x * x, SSQ_GROUPS)):
            ssq_ref[:, q * LANES:(q + 1) * LANES] = part

    @pl.when(i < n_prompt_tiles)
    def _():
        emit(xp_ref)

    @pl.when(i >= n_prompt_tiles)
    def _():
        emit(xs_ref)


def _prep(xp, xs, g):
    mp, d = xp.shape
    m = mp + xs.shape[0]
    npt = mp // PREP_ROWS
    return pl.pallas_call(
        functools.partial(_prep_kernel, npt),
        grid=(m // PREP_ROWS,),
        in_specs=[pl.BlockSpec((PREP_ROWS, d), lambda i: (jnp.minimum(i, npt - 1), 0)),
                  pl.BlockSpec((PREP_ROWS, d), lambda i: (jnp.maximum(i - npt, 0), 0)),
                  pl.BlockSpec((1, d), lambda i: (0, 0))],
        out_specs=[pl.BlockSpec((PREP_ROWS, d), lambda i: (i, 0)),
                   pl.BlockSpec((PREP_ROWS, d), lambda i: (i, 0)),
                   pl.BlockSpec((PREP_ROWS, SSQ_GROUPS * LANES), lambda i: (i, 0))],
        out_shape=[jax.ShapeDtypeStruct((m, d), F32),
                   jax.ShapeDtypeStruct((m, d), BF16),
                   jax.ShapeDtypeStruct((m, SSQ_GROUPS * LANES), F32)],
        compiler_params=_params("arbitrary"),
        name="prep",
    )(xp, xs, g)


def _final_norm_kernel(n_prompt_tiles, x_ref, g_ref, op_ref, os_ref):
    y = _rms(x_ref[...], g_ref[...])
    i = pl.program_id(0)

    @pl.when(i < n_prompt_tiles)
    def _():
        op_ref[...] = y

    @pl.when(i >= n_prompt_tiles)
    def _():
        os_ref[...] = y


def _final_norm(x, g, m_prompt):
    m, d = x.shape
    npt = m_prompt // NORM_ROWS
    return pl.pallas_call(
        functools.partial(_final_norm_kernel, npt),
        grid=(m // NORM_ROWS,),
        in_specs=[pl.BlockSpec((NORM_ROWS, d), lambda i: (i, 0)),
                  pl.BlockSpec((1, d), lambda i: (0, 0))],
        out_specs=[pl.BlockSpec((NORM_ROWS, d), lambda i: (jnp.minimum(i, npt - 1), 0)),
                   pl.BlockSpec((NORM_ROWS, d), lambda i: (jnp.maximum(i - npt, 0), 0))],
        out_shape=[jax.ShapeDtypeStruct((m_prompt, d), F32),
                   jax.ShapeDtypeStruct((m - m_prompt, d), F32)],
        compiler_params=_params("arbitrary"),
        name="final_norm",
    )(x, g)


def _weight_scratch(k):
    return [pltpu.VMEM((k, COL_TILE), BF16),
            pltpu.VMEM((k, COL_TILE), BF16),
            pltpu.VMEM((2, W_CHUNK, COL_TILE), F32),
            pltpu.SemaphoreType.DMA((2,))]


def _weight_pipeline(chunk_src, n_chunks, w_even_ref, w_odd_ref, ring_ref, sem_ref, body):
    j, i = pl.program_id(0), pl.program_id(1)

    def copy(c, jt):
        return pltpu.make_async_copy(chunk_src(c, jt), ring_ref.at[c % 2], sem_ref.at[c % 2])

    @pl.when((j == 0) & (i == 0))
    def _():
        copy(0, 0).start()
        for c in range(n_chunks):
            if c + 1 < n_chunks:
                copy(c + 1, 0).start()
            copy(c, 0).wait()
            w_even_ref[c * W_CHUNK:(c + 1) * W_CHUNK, :] = ring_ref[c % 2].astype(BF16)

    has_next = j + 1 < pl.num_programs(0)
    for c in range(n_chunks):
        @pl.when(has_next & (i == c + 1))
        def _():
            copy(c, j + 1).wait()

        @pl.when(has_next & (i == c))
        def _():
            copy(c, j + 1).start()

    def step(w_ref, next_ref):
        c = jnp.clip(i - 1, 0, n_chunks - 1)
        row0 = pl.multiple_of(c * W_CHUNK, W_CHUNK)
        body(w_ref)
        next_ref[pl.ds(row0, W_CHUNK), :] = ring_ref[(i + 1) % 2].astype(BF16)

    @pl.when(j % 2 == 0)
    def _():
        step(w_even_ref, w_odd_ref)

    @pl.when(j % 2 == 1)
    def _():
        step(w_odd_ref, w_even_ref)


def _col_window(w_hbm, layer, c, col_block):
    col0 = col_block * COL_TILE
    if not isinstance(col0, int):
        col0 = pl.multiple_of(col0, COL_TILE)
    return w_hbm.at[layer, pl.ds(c * W_CHUNK, W_CHUNK), pl.ds(col0, COL_TILE)]


N_MIX_IN = 11


def _proj_kernel(act, layer, col_block_of, emits_r, mix, xb_ref, stat_ref, w_hbm, *refs):
    refs = list(refs)
    mix_in = [refs.pop(0) for _ in range(N_MIX_IN)] if mix else []
    o_ref = refs.pop(0)
    r_ref = refs.pop(0) if emits_r else None
    mix_out = [refs.pop(0) for _ in range(3)] if mix else []
    scratch = refs
    k = xb_ref.shape[1]
    step = pl.program_id(0) * pl.num_programs(1) + pl.program_id(1)

    def row_scale():
        r = lax.rsqrt(jnp.sum(stat_ref[...], axis=-1, keepdims=True) / k + EPS)
        return jnp.broadcast_to(r, (r.shape[0], LANES))

    if emits_r:
        @pl.when(pl.program_id(0) == 0)
        def _():
            r_ref[...] = row_scale()

    if mix:
        tiles_per_seq, n_tiles = mix

        @pl.when(step < n_tiles)
        def _():
            _prompt_mix(step % tiles_per_seq, tiles_per_seq, *mix_in, *mix_out, *scratch[4:])

    def body(w_ref):
        r_lanes = row_scale() if emits_r else stat_ref[...]
        r_cols = jnp.concatenate([r_lanes] * (DOT_COLS // LANES), axis=1)
        for q in range(COL_TILE // DOT_COLS):
            cs = slice(q * DOT_COLS, (q + 1) * DOT_COLS)
            p = jnp.dot(xb_ref[...], w_ref[:, cs], preferred_element_type=F32) * r_cols
            if act != "none":
                gate = 0.5 * jnp.tanh(0.5 * p) + 0.5
                p = p * gate if act == "silu" else gate
            o_ref[:, cs] = p.astype(o_ref.dtype)

    _weight_pipeline(lambda c, jt: _col_window(w_hbm, layer, c, col_block_of(jt)),
                     k // W_CHUNK, *scratch[:4], body)


def _proj(xb, stat, w_in, layer, col_block_of, n_col_blocks, act, out_dtype, emits_r=False, mix_args=None):
    m, k = xb.shape
    n_rows = m // ROW_TILE
    assert n_rows >= k // W_CHUNK + 1
    tile = pl.BlockSpec((ROW_TILE, COL_TILE), lambda j, i: (i, j))
    in_specs = [pl.BlockSpec((ROW_TILE, k), lambda j, i: (i, 0)),
                pl.BlockSpec((ROW_TILE, stat.shape[1]), lambda j, i: (i, 0)),
                pl.BlockSpec(memory_space=pl.ANY)]
    args = [xb, stat, w_in]
    out_specs, out_shape = [tile], [jax.ShapeDtypeStruct((m, n_col_blocks * COL_TILE), out_dtype)]
    scratch = _weight_scratch(k)
    if emits_r:
        out_specs.append(pl.BlockSpec((ROW_TILE, LANES), lambda j, i: (jnp.where(j == 0, i, n_rows - 1), 0)))
        out_shape.append(jax.ShapeDtypeStruct((m, LANES), F32))
    mix = None
    if mix_args is not None:
        auv, gates, pw, ps, sg, sw, sbt, batch, seq = mix_args
        tiles_per_seq = seq // MIX_ROWS
        mix = (tiles_per_seq, batch * tiles_per_seq)
        assert n_col_blocks * n_rows >= mix[1]
        tile_of = lambda j, i: jnp.minimum(j * n_rows + i, mix[1] - 1)
        mi, margs, mo, mshape, mscratch = _prompt_mix_specs(auv, gates, pw, ps, sg, sw, sbt, layer, batch,
                                                             tiles_per_seq, tile_of)
        in_specs += mi
        args += margs
        out_specs += mo
        out_shape += mshape
        scratch = scratch + mscratch
    return pl.pallas_call(
        functools.partial(_proj_kernel, act, layer, col_block_of, emits_r, mix),
        grid=(n_col_blocks, n_rows),
        in_specs=in_specs,
        out_specs=out_specs,
        out_shape=out_shape,
        scratch_shapes=scratch,
        compiler_params=_params("arbitrary", "arbitrary"),
        name="in_proj_" + act,
    )(*args)


def _tril_bf16(w):
    n = w.shape[-1]
    row = lax.broadcasted_iota(jnp.int32, (n, n), 0)
    col = lax.broadcasted_iota(jnp.int32, (n, n), 1)
    return jnp.where(col <= row, w, 0.0).astype(BF16)


def _prompt_mix(s, tiles_per_seq, a_ref, halo_ref, u_ref, v_ref, ga_ref, gb_ref, pw_ref, ps_ref, sg_ref,
                sw_ref, sbt_ref, ya_ref, yb_ref, st_ref, f_ref, p_ref, q_ref):
    rows, width = a_ref.shape
    gdim = width // POOL_GROUPS
    hdim = width // SGU_HEADS
    end = MIX_FRONT + rows

    f_ref[0:MIX_FRONT - HALO, :] = jnp.zeros((MIX_FRONT - HALO, width), F32)
    f_ref[MIX_FRONT - HALO:MIX_FRONT, :] = jnp.where(s == 0, 0.0, halo_ref[...])
    f_ref[MIX_FRONT:, :] = a_ref[...]
    pos = s * rows + lax.broadcasted_iota(jnp.int32, (rows, 1), 0)
    for g, w in enumerate(POOL_WINDOWS):
        cs = slice(g * gdim, (g + 1) * gdim)
        tok = a_ref[:, cs]
        stages = w.bit_length() - 1
        assert 1 << stages == w and w // 2 <= SUBLANES and SUBLANES * (stages - 1) < MIX_FRONT
        src, cols = f_ref, cs
        for j in range(1, stages + 1):
            shift = 1 << (j - 1)
            start = MIX_FRONT - SUBLANES * (stages - j)
            acc = src[start:end, cols] + src[start - shift:end - shift, cols]
            if j < stages:
                dst = (p_ref, q_ref)[j % 2]
                dst[start:end, :] = acc
                src, cols = dst, slice(0, gdim)
        cnt = jnp.minimum(pos + 1, w).astype(F32)
        pooled = (acc / cnt - tok).astype(BF16)
        mixed = jnp.dot(pooled, pw_ref[g], preferred_element_type=F32)
        ya_ref[:, cs] = (mixed * ps_ref[:, cs] * ga_ref[:, cs].astype(F32)).astype(BF16)

    vn = _rms(v_ref[...], sg_ref[...]).astype(BF16)
    for h in range(SGU_HEADS):
        cs = slice(h * hdim, (h + 1) * hdim)
        wsh = _tril_bf16(sw_ref[h])
        bias = sbt_ref[:, h:h + 1]
        for c in range(rows // CHUNK):
            rs = slice(c * CHUNK, (c + 1) * CHUNK)
            z = jnp.dot(wsh, vn[rs, cs], preferred_element_type=F32) + bias
            yb_ref[rs, cs] = (u_ref[rs, cs] * z * gb_ref[rs, cs].astype(F32)).astype(BF16)

    @pl.when(s == tiles_per_seq - 1)
    def _():
        st_ref[...] = f_ref[end - POOL_BUF:end, :]


def _prompt_mix_specs(auv, gates, pw, ps, sg, sw, sbt, layer, batch, tiles_per_seq, tile_of):
    m, width3 = auv.shape
    width = width3 // 3
    halo_per_tile = MIX_ROWS // HALO
    once = pl.Buffered(1)
    rows = lambda col: pl.BlockSpec((MIX_ROWS, width), lambda *g: (tile_of(*g), col))
    in_specs = [
        rows(0),
        pl.BlockSpec((HALO, width), lambda *g: (jnp.maximum(tile_of(*g) * halo_per_tile - 1, 0), 0)),
        rows(1), rows(2), rows(0), rows(1),
        pl.BlockSpec((None,) + pw.shape[1:], lambda *g: (layer, 0, 0, 0), pipeline_mode=once),
        pl.BlockSpec((None, 1, width), lambda *g: (layer, 0, 0)),
        pl.BlockSpec((None, 1, width), lambda *g: (layer, 0, 0)),
        pl.BlockSpec((None,) + sw.shape[1:], lambda *g: (layer, 0, 0, 0), pipeline_mode=once),
        pl.BlockSpec((None,) + sbt.shape[1:], lambda *g: (layer, 0, 0)),
    ]
    assert len(in_specs) == N_MIX_IN
    args = [auv, auv, auv, auv, gates, gates, pw, ps, sg, sw, sbt]
    out_specs = [rows(0), rows(0),
                 pl.BlockSpec((None, POOL_BUF, width), lambda *g: (tile_of(*g) // tiles_per_seq, 0, 0))]
    out_shape = [jax.ShapeDtypeStruct((m, width), BF16),
                 jax.ShapeDtypeStruct((m, width), BF16),
                 jax.ShapeDtypeStruct((batch, POOL_BUF, width), F32)]
    scratch = [pltpu.VMEM((MIX_FRONT + MIX_ROWS, width), F32),
               pltpu.VMEM((MIX_FRONT + MIX_ROWS, width // POOL_GROUPS), F32),
               pltpu.VMEM((MIX_FRONT + MIX_ROWS, width // POOL_GROUPS), F32)]
    return in_specs, args, out_specs, out_shape, scratch


def _mix_sample_kernel(n_aliased, a_ref, u_ref, v_ref, ga_ref, gb_ref, buf_ref, pw_ref, ps_ref, sg_ref,
                       sw_ref, *refs):
    ya_ref, yb_ref, nb_ref, vn_ref = refs[n_aliased:]
    rows, width = a_ref.shape
    nseq = nb_ref.shape[1]
    steps = rows // nseq
    gdim = width // POOL_GROUPS
    slab = lambda t: slice(t * nseq, (t + 1) * nseq)

    def full(i, cs):
        if i < POOL_BUF:
            return buf_ref[i, :, cs]
        return a_ref[slab(i - POOL_BUF), cs]

    for r in range(POOL_BUF):
        nb_ref[r] = full(steps + r, slice(0, width))

    for t in range(steps):
        for g, w in enumerate(POOL_WINDOWS):
            cs = slice(g * gdim, (g + 1) * gdim)
            tok = full(POOL_BUF + t, cs)
            acc = tok
            for k in range(1, w):
                acc = acc + full(POOL_BUF + t - k, cs)
            cnt = float(min(PAST_LEN + t + 1, w))
            pooled = (acc / cnt - tok).astype(BF16)
            mixed = jnp.dot(pooled, pw_ref[g], preferred_element_type=F32)
            ya_ref[slab(t), cs] = (mixed * ps_ref[:, cs] * ga_ref[slab(t), cs].astype(F32)).astype(BF16)

    vn = [_rms(v_ref[slab(t), :], sg_ref[...]) for t in range(steps)]
    for t in range(steps):
        vn_ref[:, t, :] = vn[t]
    vnb = [x.astype(BF16).astype(F32) for x in vn]
    for t in range(steps):
        z = sw_ref[steps, t:t + 1, :]
        for s in range(t + 1):
            z = z + sw_ref[s, t:t + 1, :] * vnb[s]
        yb_ref[slab(t), :] = (u_ref[slab(t), :] * z * gb_ref[slab(t), :].astype(F32)).astype(BF16)


def _mix_sample(auv, gates, ya, yb, state_pool, carried, pw, ps, sg, swx, layer, m_prompt, steps):
    m, width3 = auv.shape
    width = width3 // 3
    depth, _, nseq, _ = state_pool.shape
    rows = MIX_SEQS * steps
    first = m_prompt // rows
    row = lambda i: first + i
    any_spec = pl.BlockSpec(memory_space=pl.ANY)
    aliased = (ya, yb) + (tuple(carried) if carried is not None else ())
    blocked = [
        (auv, pl.BlockSpec((rows, width), lambda i: (row(i), 0))),
        (auv, pl.BlockSpec((rows, width), lambda i: (row(i), 1))),
        (auv, pl.BlockSpec((rows, width), lambda i: (row(i), 2))),
        (gates, pl.BlockSpec((rows, width), lambda i: (row(i), 0))),
        (gates, pl.BlockSpec((rows, width), lambda i: (row(i), 1))),
        (state_pool, pl.BlockSpec((None, POOL_BUF, MIX_SEQS, width), lambda i: (layer, 0, i, 0))),
        (pw, pl.BlockSpec((None,) + pw.shape[1:], lambda i: (layer, 0, 0, 0))),
        (ps, pl.BlockSpec((None, 1, width), lambda i: (layer, 0, 0))),
        (sg, pl.BlockSpec((None, 1, width), lambda i: (layer, 0, 0))),
        (swx, pl.BlockSpec((None,) + swx.shape[1:], lambda i: (layer, 0, 0, 0))),
    ]
    return pl.pallas_call(
        functools.partial(_mix_sample_kernel, len(aliased)),
        grid=(nseq // MIX_SEQS,),
        in_specs=[spec for _, spec in blocked] + [any_spec] * len(aliased),
        out_specs=[
            pl.BlockSpec((rows, width), lambda i: (row(i), 0)),
            pl.BlockSpec((rows, width), lambda i: (row(i), 0)),
            pl.BlockSpec((None, POOL_BUF, MIX_SEQS, width), lambda i: (layer, 0, i, 0)),
            pl.BlockSpec((None, MIX_SEQS, steps, width), lambda i: (layer, i, 0, 0)),
        ],
        out_shape=[jax.ShapeDtypeStruct(ya.shape, ya.dtype),
                   jax.ShapeDtypeStruct(yb.shape, yb.dtype),
                   jax.ShapeDtypeStruct((depth, POOL_BUF, nseq, width), F32),
                   jax.ShapeDtypeStruct((depth, nseq, steps, width), F32)],
        input_output_aliases={len(blocked) + k: k for k in range(len(aliased))},
        compiler_params=_params("arbitrary"),
        name="mix_sample",
    )(*[arr for arr, _ in blocked], *aliased)


def _merge_kernel(layer, ya_ref, yb_ref, wa_hbm, wb_hbm, sa_ref, sb_ref, o_ref, *scratch):
    k = ya_ref.shape[1]
    per = k // W_CHUNK

    def chunk_src(c, jt):
        if c < per:
            return _col_window(wa_hbm, layer, c, jt)
        return _col_window(wb_hbm, layer, c - per, jt)

    def body(w_ref):
        for q in range(COL_TILE // DOT_COLS):
            cs = slice(q * DOT_COLS, (q + 1) * DOT_COLS)
            pa = jnp.dot(ya_ref[...], w_ref[0:k, cs], preferred_element_type=F32)
            pb = jnp.dot(yb_ref[...], w_ref[k:2 * k, cs], preferred_element_type=F32)
            o_ref[:, cs] = (sa_ref[:, cs].astype(F32) * pa + sb_ref[:, cs].astype(F32) * pb).astype(o_ref.dtype)

    _weight_pipeline(chunk_src, 2 * per, *scratch, body)


def _merge(ya, yb, wa, wb, sel, layer):
    m, k = ya.shape
    n = wa.shape[-1]
    nb = n // COL_TILE
    assert m // ROW_TILE >= 2 * k // W_CHUNK + 1
    any_spec = pl.BlockSpec(memory_space=pl.ANY)
    return pl.pallas_call(
        functools.partial(_merge_kernel, layer),
        grid=(nb, m // ROW_TILE),
        in_specs=[pl.BlockSpec((ROW_TILE, k), lambda j, i: (i, 0)),
                  pl.BlockSpec((ROW_TILE, k), lambda j, i: (i, 0)),
                  any_spec, any_spec,
                  pl.BlockSpec((ROW_TILE, COL_TILE), lambda j, i: (i, j)),
                  pl.BlockSpec((ROW_TILE, COL_TILE), lambda j, i: (i, j + nb))],
        out_specs=pl.BlockSpec((ROW_TILE, COL_TILE), lambda j, i: (i, j)),
        out_shape=jax.ShapeDtypeStruct((m, n), BF16),
        scratch_shapes=_weight_scratch(2 * k),
        compiler_params=_params("arbitrary", "arbitrary"),
        name="branch_merge",
    )(ya, yb, wa, wb, sel, sel)


def _out_proj_kernel(layer, feeds_next, mg_ref, w_hbm, x_ref, *refs):
    if feeds_next:
        g_ref, o_ref, xb_ref, ssq_ref = refs[:4]
    else:
        o_ref = refs[0]
    k = mg_ref.shape[1]

    def body(w_ref):
        parts = []
        for q in range(COL_TILE // DOT_COLS):
            cs = slice(q * DOT_COLS, (q + 1) * DOT_COLS)
            y = x_ref[:, cs] + jnp.dot(mg_ref[...], w_ref[:, cs], preferred_element_type=F32)
            o_ref[:, cs] = y
            if feeds_next:
                xb_ref[:, cs] = (y * g_ref[:, cs]).astype(BF16)
                parts += _lane_partials(y * y, 1)
        if feeds_next:
            ssq_ref[...] = functools.reduce(lambda a, b: a + b, parts)

    _weight_pipeline(lambda c, jt: _col_window(w_hbm, layer, c, jt), k // W_CHUNK, *refs[-4:], body)


def _out_proj(merged, w_out, x, layer, g_all):
    m, k = merged.shape
    n = w_out.shape[-1]
    assert m // ROW_TILE >= k // W_CHUNK + 1
    feeds_next = g_all is not None
    tile = pl.BlockSpec((ROW_TILE, COL_TILE), lambda j, i: (i, j))
    in_specs = [pl.BlockSpec((ROW_TILE, k), lambda j, i: (i, 0)), pl.BlockSpec(memory_space=pl.ANY), tile]
    args = [merged, w_out, x]
    out_specs, out_shape = [tile], [jax.ShapeDtypeStruct((m, n), F32)]
    if feeds_next:
        assert n // COL_TILE == SSQ_GROUPS
        in_specs.append(pl.BlockSpec((None, 1, COL_TILE), lambda j, i: (layer + 1, 0, j)))
        args.append(g_all)
        out_specs += [tile, pl.BlockSpec((ROW_TILE, LANES), lambda j, i: (i, j))]
        out_shape += [jax.ShapeDtypeStruct((m, n), BF16), jax.ShapeDtypeStruct((m, SSQ_GROUPS * LANES), F32)]
    return pl.pallas_call(
        functools.partial(_out_proj_kernel, layer, feeds_next),
        grid=(n // COL_TILE, m // ROW_TILE),
        in_specs=in_specs,
        out_specs=out_specs,
        out_shape=out_shape,
        scratch_shapes=_weight_scratch(k),
        compiler_params=_params("arbitrary", "arbitrary"),
        name="out_proj",
    )(*args)


def kernel(x_prompt, x_sample, state_pool, norm_g, w_in, pool_w, pool_scale, sgu_norm_g, sgu_w, sgu_b,
           w_branch_a, w_branch_b, w_out, final_norm_g):
    batch, seq, d = x_prompt.shape
    nseq, steps, _ = x_sample.shape
    depth = w_in.shape[0]
    width = pool_scale.shape[-1]
    hdim = width // SGU_HEADS
    m_prompt = batch * seq
    assert PAST_LEN % CHUNK == 0 and steps <= CHUNK
    wt = width // COL_TILE
    ident_cols = lambda j: jnp.where(j < wt, j, j + wt)
    silu_cols = lambda j: jnp.where(j < wt, j + wt, j + 3 * wt)
    sigm_cols = lambda j: j + 5 * wt

    seq_tiles = nseq // MIX_SEQS
    xs = jnp.swapaxes(x_sample.reshape(seq_tiles, MIX_SEQS, steps, d), 1, 2).reshape(nseq * steps, d)
    g_all = norm_g.reshape(depth, 1, d)
    x, xb, ssq = _prep(x_prompt.reshape(m_prompt, d), xs, g_all[0])
    pw = pool_w.astype(BF16)
    ps = pool_scale.reshape(depth, 1, width)
    sg = sgu_norm_g.reshape(depth, 1, width)
    sbt = jnp.swapaxes(sgu_b, 1, 2)
    sw_small = sgu_w[:, :, :steps, :steps].astype(BF16).astype(F32)
    swx = jnp.repeat(jnp.transpose(sw_small, (0, 3, 2, 1)), hdim, axis=-1)
    bx = jnp.repeat(jnp.swapaxes(sgu_b[:, :, :steps], 1, 2), hdim, axis=-1)[:, None]
    swx = jnp.concatenate([swx, bx], axis=1)

    pool_t = jnp.swapaxes(state_pool, 1, 2)
    pool_p, carried = [], None
    for l in range(depth):
        auv, r = _proj(xb, ssq, w_in, l, ident_cols, 3 * wt, "none", F32, emits_r=True)
        gates, = _proj(xb, r, w_in, l, silu_cols, 2 * wt, "silu", BF16)
        sel, ya, yb, st_p = _proj(xb, r, w_in, l, sigm_cols, 2 * (d // COL_TILE), "sigmoid", BF16,
                                  mix_args=(auv, gates, pw, ps, sg, sgu_w, sbt, batch, seq))
        ya, yb, *carried = _mix_sample(auv, gates, ya, yb, pool_t, carried, pw, ps, sg, swx, l,
                                       m_prompt, steps)
        merged = _merge(ya, yb, w_branch_a, w_branch_b, sel, l)
        if l + 1 < depth:
            x, xb, ssq = _out_proj(merged, w_out, x, l, g_all)
        else:
            x, = _out_proj(merged, w_out, x, l, None)
        pool_p.append(st_p)

    y_p, y_s = _final_norm(x, final_norm_g.reshape(1, d), m_prompt)
    y_s = jnp.swapaxes(y_s.reshape(seq_tiles, steps, MIX_SEQS, d), 1, 2)
    return (y_p.reshape(batch, seq, d), y_s.reshape(nseq, steps, d),
            jnp.stack(pool_p, axis=0), jnp.swapaxes(carried[0], 1, 2), carried[1])
```

```python
import functools

import jax
import jax.numpy as jnp
from jax import lax
from jax.experimental import pallas as pl
from jax.experimental.pallas import tpu as pltpu

F32 = jnp.float32
BF16 = jnp.bfloat16

POOL_WINDOWS = (2, 4, 8, 16)
POOL_GROUPS = len(POOL_WINDOWS)
POOL_BUF = max(POOL_WINDOWS) - 1
HALO = POOL_BUF + 1
SGU_HEADS = 8
CHUNK = 128
EPS = 1e-6
PAST_LEN = 16384

V7X_VMEM_LIMIT_BYTES = 60 * 1024 * 1024

LANES = 128
SUBLANES = 8
ROW_TILE = 1024
COL_TILE = 1024
DOT_COLS = 256
W_CHUNK = 512
SSQ_GROUPS = 4
PREP_ROWS = 256
NORM_ROWS = 512
MIX_ROWS = 128
MIX_FRONT = 2 * HALO
MIX_SEQS = 32


def _params(*semantics):
    return pltpu.CompilerParams(dimension_semantics=semantics,
                                vmem_limit_bytes=V7X_VMEM_LIMIT_BYTES)


def _rms(x, g):
    ms = jnp.mean(x * x, axis=-1, keepdims=True)
    return x * lax.rsqrt(ms + EPS) * g


def _lane_partials(sq, groups):
    nblk = sq.shape[1] // LANES
    per = nblk // groups
    outs = []
    for q in range(groups):
        acc = sq[:, q * per * LANES:(q * per + 1) * LANES]
        for b in range(q * per + 1, (q + 1) * per):
            acc = acc + sq[:, b * LANES:(b + 1) * LANES]
        outs.append(acc)
    return outs


def _prep_kernel(n_prompt_tiles, xp_ref, xs_ref, g_ref, x_ref, xb_ref, ssq_ref):
    i = pl.program_id(0)

    def emit(src_ref):
        x = src_ref[...]
        x_ref[...] = x
        xb_ref[...] = (x * g_ref[...]).astype(BF16)
        for q, part in enumerate(_lane_partials(x * x, SSQ_GROUPS)):
            ssq_ref[:, q * LANES:(q + 1) * LANES] = part

    @pl.when(i < n_prompt_tiles)
    def _():
        emit(xp_ref)

    @pl.when(i >= n_prompt_tiles)
    def _():
        emit(xs_ref)


def _prep(xp, xs, g):
    mp, d = xp.shape
    m = mp + xs.shape[0]
    npt = mp // PREP_ROWS
    return pl.pallas_call(
        functools.partial(_prep_kernel, npt),
        grid=(m // PREP_ROWS,),
        in_specs=[pl.BlockSpec((PREP_ROWS, d), lambda i: (jnp.minimum(i, npt - 1), 0)),
                  pl.BlockSpec((PREP_ROWS, d), lambda i: (jnp.maximum(i - npt, 0), 0)),
                  pl.BlockSpec((1, d), lambda i: (0, 0))],
        out_specs=[pl.BlockSpec((PREP_ROWS, d), lambda i: (i, 0)),
                   pl.BlockSpec((PREP_ROWS, d), lambda i: (i, 0)),
                   pl.BlockSpec((PREP_ROWS, SSQ_GROUPS * LANES), lambda i: (i, 0))],
        out_shape=[jax.ShapeDtypeStruct((m, d), F32),
                   jax.ShapeDtypeStruct((m, d), BF16),
                   jax.ShapeDtypeStruct((m, SSQ_GROUPS * LANES), F32)],
        compiler_params=_params("arbitrary"),
        name="prep",
    )(xp, xs, g)


def _final_norm_kernel(n_prompt_tiles, x_ref, g_ref, op_ref, os_ref):
    y = _rms(x_ref[...], g_ref[...])
    i = pl.program_id(0)

    @pl.when(i < n_prompt_tiles)
    def _():
        op_ref[...] = y

    @pl.when(i >= n_prompt_tiles)
    def _():
        os_ref[...] = y


def _final_norm(x, g, m_prompt):
    m, d = x.shape
    npt = m_prompt // NORM_ROWS
    return pl.pallas_call(
        functools.partial(_final_norm_kernel, npt),
        grid=(m // NORM_ROWS,),
        in_specs=[pl.BlockSpec((NORM_ROWS, d), lambda i: (i, 0)),
                  pl.BlockSpec((1, d), lambda i: (0, 0))],
        out_specs=[pl.BlockSpec((NORM_ROWS, d), lambda i: (jnp.minimum(i, npt - 1), 0)),
                   pl.BlockSpec((NORM_ROWS, d), lambda i: (jnp.maximum(i - npt, 0), 0))],
        out_shape=[jax.ShapeDtypeStruct((m_prompt, d), F32),
                   jax.ShapeDtypeStruct((m - m_prompt, d), F32)],
        compiler_params=_params("arbitrary"),
        name="final_norm",
    )(x, g)


def _weight_scratch(k):
    return [pltpu.VMEM((k, COL_TILE), BF16),
            pltpu.VMEM((k, COL_TILE), BF16),
            pltpu.VMEM((2, W_CHUNK, COL_TILE), F32),
            pltpu.SemaphoreType.DMA((2,))]


def _weight_pipeline(chunk_src, n_chunks, w_even_ref, w_odd_ref, ring_ref, sem_ref, body):
    j, i = pl.program_id(0), pl.program_id(1)

    def copy(c, jt):
        return pltpu.make_async_copy(chunk_src(c, jt), ring_ref.at[c % 2], sem_ref.at[c % 2])

    @pl.when((j == 0) & (i == 0))
    def _():
        copy(0, 0).start()
        for c in range(n_chunks):
            if c + 1 < n_chunks:
                copy(c + 1, 0).start()
            copy(c, 0).wait()
            w_even_ref[c * W_CHUNK:(c + 1) * W_CHUNK, :] = ring_ref[c % 2].astype(BF16)

    has_next = j + 1 < pl.num_programs(0)
    for c in range(n_chunks):
        @pl.when(has_next & (i == c + 1))
        def _():
            copy(c, j + 1).wait()

        @pl.when(has_next & (i == c))
        def _():
            copy(c, j + 1).start()

    def step(w_ref, next_ref):
        c = jnp.clip(i - 1, 0, n_chunks - 1)
        row0 = pl.multiple_of(c * W_CHUNK, W_CHUNK)
        body(w_ref)
        next_ref[pl.ds(row0, W_CHUNK), :] = ring_ref[(i + 1) % 2].astype(BF16)

    @pl.when(j % 2 == 0)
    def _():
        step(w_even_ref, w_odd_ref)

    @pl.when(j % 2 == 1)
    def _():
        step(w_odd_ref, w_even_ref)


def _col_window(w_hbm, layer, c, col_block):
    col0 = col_block * COL_TILE
    if not isinstance(col0, int):
        col0 = pl.multiple_of(col0, COL_TILE)
    return w_hbm.at[layer, pl.ds(c * W_CHUNK, W_CHUNK), pl.ds(col0, COL_TILE)]


N_MIX_IN = 11


def _proj_kernel(act, layer, col_block_of, emits_r, mix, xb_ref, stat_ref, w_hbm, *refs):
    refs = list(refs)
    mix_in = [refs.pop(0) for _ in range(N_MIX_IN)] if mix else []
    o_ref = refs.pop(0)
    r_ref = refs.pop(0) if emits_r else None
    mix_out = [refs.pop(0) for _ in range(3)] if mix else []
    scratch = refs
    k = xb_ref.shape[1]
    step = pl.program_id(0) * pl.num_programs(1) + pl.program_id(1)

    def row_scale():
        r = lax.rsqrt(jnp.sum(stat_ref[...], axis=-1, keepdims=True) / k + EPS)
        return jnp.broadcast_to(r, (r.shape[0], LANES))

    row_step = pl.program_id(1)
    if emits_r:
        rs_ref = scratch[4]

        @pl.when(pl.program_id(0) == 0)
        def _():
            r_lanes = row_scale()
            r_ref[...] = r_lanes
            rs_ref[row_step] = r_lanes

    if mix:
        tiles_per_seq, n_tiles = mix

        @pl.when(step < n_tiles)
        def _():
            _prompt_mix(step % tiles_per_seq, tiles_per_seq, *mix_in, *mix_out, *scratch[4:])

    def body(w_ref):
        r_lanes = rs_ref[row_step] if emits_r else stat_ref[...]
        r_cols = jnp.concatenate([r_lanes] * (DOT_COLS // LANES), axis=1)
        for q in range(COL_TILE // DOT_COLS):
            cs = slice(q * DOT_COLS, (q + 1) * DOT_COLS)
            p = jnp.dot(xb_ref[...], w_ref[:, cs], preferred_element_type=F32) * r_cols
            if act != "none":
                gate = 0.5 * jnp.tanh(0.5 * p) + 0.5
                p = p * gate if act == "silu" else gate
            o_ref[:, cs] = p.astype(o_ref.dtype)

    _weight_pipeline(lambda c, jt: _col_window(w_hbm, layer, c, col_block_of(jt)),
                     k // W_CHUNK, *scratch[:4], body)


def _proj(xb, stat, w_in, layer, col_block_of, n_col_blocks, act, out_dtype, emits_r=False, mix_args=None):
    m, k = xb.shape
    n_rows = m // ROW_TILE
    assert n_rows >= k // W_CHUNK + 1
    assert not (emits_r and mix_args is not None)
    tile = pl.BlockSpec((ROW_TILE, COL_TILE), lambda j, i: (i, j))
    in_specs = [pl.BlockSpec((ROW_TILE, k), lambda j, i: (i, 0)),
                pl.BlockSpec((ROW_TILE, stat.shape[1]), lambda j, i: (i, 0)),
                pl.BlockSpec(memory_space=pl.ANY)]
    args = [xb, stat, w_in]
    out_specs, out_shape = [tile], [jax.ShapeDtypeStruct((m, n_col_blocks * COL_TILE), out_dtype)]
    scratch = _weight_scratch(k)
    if emits_r:
        out_specs.append(pl.BlockSpec((ROW_TILE, LANES), lambda j, i: (jnp.where(j == 0, i, n_rows - 1), 0)))
        out_shape.append(jax.ShapeDtypeStruct((m, LANES), F32))
        scratch = scratch + [pltpu.VMEM((n_rows, ROW_TILE, LANES), F32)]
    mix = None
    if mix_args is not None:
        auv, gates, pw, ps, sg, sw, sbt, batch, seq = mix_args
        tiles_per_seq = seq // MIX_ROWS
        mix = (tiles_per_seq, batch * tiles_per_seq)
        assert n_col_blocks * n_rows >= mix[1]
        tile_of = lambda j, i: jnp.minimum(j * n_rows + i, mix[1] - 1)
        mi, margs, mo, mshape, mscratch = _prompt_mix_specs(auv, gates, pw, ps, sg, sw, sbt, layer, batch,
                                                             tiles_per_seq, tile_of)
        in_specs += mi
        args += margs
        out_specs += mo
        out_shape += mshape
        scratch = scratch + mscratch
    return pl.pallas_call(
        functools.partial(_proj_kernel, act, layer, col_block_of, emits_r, mix),
        grid=(n_col_blocks, n_rows),
        in_specs=in_specs,
        out_specs=out_specs,
        out_shape=out_shape,
        scratch_shapes=scratch,
        compiler_params=_params("arbitrary", "arbitrary"),
        name="in_proj_" + act,
    )(*args)


def _tril_bf16(w):
    n = w.shape[-1]
    row = lax.broadcasted_iota(jnp.int32, (n, n), 0)
    col = lax.broadcasted_iota(jnp.int32, (n, n), 1)
    return jnp.where(col <= row, w, 0.0).astype(BF16)


def _prompt_mix(s, tiles_per_seq, a_ref, halo_ref, u_ref, v_ref, ga_ref, gb_ref, pw_ref, ps_ref, sg_ref,
                sw_ref, sbt_ref, ya_ref, yb_ref, st_ref, f_ref, p_ref, q_ref):
    rows, width = a_ref.shape
    gdim = width // POOL_GROUPS
    hdim = width // SGU_HEADS
    end = MIX_FRONT + rows

    f_ref[0:MIX_FRONT - HALO, :] = jnp.zeros((MIX_FRONT - HALO, width), F32)
    f_ref[MIX_FRONT - HALO:MIX_FRONT, :] = jnp.where(s == 0, 0.0, halo_ref[...])
    f_ref[MIX_FRONT:, :] = a_ref[...]
    pos = s * rows + lax.broadcasted_iota(jnp.int32, (rows, 1), 0)
    for g, w in enumerate(POOL_WINDOWS):
        cs = slice(g * gdim, (g + 1) * gdim)
        tok = a_ref[:, cs]
        stages = w.bit_length() - 1
        assert 1 << stages == w and w // 2 <= SUBLANES and SUBLANES * (stages - 1) < MIX_FRONT
        src, cols = f_ref, cs
        for j in range(1, stages + 1):
            shift = 1 << (j - 1)
            start = MIX_FRONT - SUBLANES * (stages - j)
            acc = src[start:end, cols] + src[start - shift:end - shift, cols]
            if j < stages:
                dst = (p_ref, q_ref)[j % 2]
                dst[start:end, :] = acc
                src, cols = dst, slice(0, gdim)
        cnt = jnp.minimum(pos + 1, w).astype(F32)
        pooled = (acc / cnt - tok).astype(BF16)
        mixed = jnp.dot(pooled, pw_ref[g], preferred_element_type=F32)
        ya_ref[:, cs] = (mixed * ps_ref[:, cs] * ga_ref[:, cs].astype(F32)).astype(BF16)

    vn = _rms(v_ref[...], sg_ref[...]).astype(BF16)
    for h in range(SGU_HEADS):
        cs = slice(h * hdim, (h + 1) * hdim)
        wsh = _tril_bf16(sw_ref[h])
        bias = sbt_ref[:, h:h + 1]
        for c in range(rows // CHUNK):
            rs = slice(c * CHUNK, (c + 1) * CHUNK)
            z = jnp.dot(wsh, vn[rs, cs], preferred_element_type=F32) + bias
            yb_ref[rs, cs] = (u_ref[rs, cs] * z * gb_ref[rs, cs].astype(F32)).astype(BF16)

    @pl.when(s == tiles_per_seq - 1)
    def _():
        st_ref[...] = f_ref[end - POOL_BUF:end, :]


def _prompt_mix_specs(auv, gates, pw, ps, sg, sw, sbt, layer, batch, tiles_per_seq, tile_of):
    m, width3 = auv.shape
    width = width3 // 3
    halo_per_tile = MIX_ROWS // HALO
    once = pl.Buffered(1)
    rows = lambda col: pl.BlockSpec((MIX_ROWS, width), lambda *g: (tile_of(*g), col))
    in_specs = [
        rows(0),
        pl.BlockSpec((HALO, width), lambda *g: (jnp.maximum(tile_of(*g) * halo_per_tile - 1, 0), 0)),
        rows(1), rows(2), rows(0), rows(1),
        pl.BlockSpec((None,) + pw.shape[1:], lambda *g: (layer, 0, 0, 0), pipeline_mode=once),
        pl.BlockSpec((None, 1, width), lambda *g: (layer, 0, 0)),
        pl.BlockSpec((None, 1, width), lambda *g: (layer, 0, 0)),
        pl.BlockSpec((None,) + sw.shape[1:], lambda *g: (layer, 0, 0, 0), pipeline_mode=once),
        pl.BlockSpec((None,) + sbt.shape[1:], lambda *g: (layer, 0, 0)),
    ]
    assert len(in_specs) == N_MIX_IN
    args = [auv, auv, auv, auv, gates, gates, pw, ps, sg, sw, sbt]
    out_specs = [rows(0), rows(0),
                 pl.BlockSpec((None, POOL_BUF, width), lambda *g: (tile_of(*g) // tiles_per_seq, 0, 0))]
    out_shape = [jax.ShapeDtypeStruct((m, width), BF16),
                 jax.ShapeDtypeStruct((m, width), BF16),
                 jax.ShapeDtypeStruct((batch, POOL_BUF, width), F32)]
    scratch = [pltpu.VMEM((MIX_FRONT + MIX_ROWS, width), F32),
               pltpu.VMEM((MIX_FRONT + MIX_ROWS, width // POOL_GROUPS), F32),
               pltpu.VMEM((MIX_FRONT + MIX_ROWS, width // POOL_GROUPS), F32)]
    return in_specs, args, out_specs, out_shape, scratch


def _mix_sample_kernel(n_aliased, a_ref, u_ref, v_ref, ga_ref, gb_ref, buf_ref, pw_ref, ps_ref, sg_ref,
                       sw_ref, *refs):
    ya_ref, yb_ref, nb_ref, vn_ref = refs[n_aliased:]
    rows, width = a_ref.shape
    nseq = nb_ref.shape[1]
    steps = rows // nseq
    gdim = width // POOL_GROUPS
    slab = lambda t: slice(t * nseq, (t + 1) * nseq)

    def full(i, cs):
        if i < POOL_BUF:
            return buf_ref[i, :, cs]
        return a_ref[slab(i - POOL_BUF), cs]

    for r in range(POOL_BUF):
        nb_ref[r] = full(steps + r, slice(0, width))

    for t in range(steps):
        for g, w in enumerate(POOL_WINDOWS):
            cs = slice(g * gdim, (g + 1) * gdim)
            tok = full(POOL_BUF + t, cs)
            acc = tok
            for k in range(1, w):
                acc = acc + full(POOL_BUF + t - k, cs)
            cnt = float(min(PAST_LEN + t + 1, w))
            pooled = (acc / cnt - tok).astype(BF16)
            mixed = jnp.dot(pooled, pw_ref[g], preferred_element_type=F32)
            ya_ref[slab(t), cs] = (mixed * ps_ref[:, cs] * ga_ref[slab(t), cs].astype(F32)).astype(BF16)

    vn = [_rms(v_ref[slab(t), :], sg_ref[...]) for t in range(steps)]
    for t in range(steps):
        vn_ref[:, t, :] = vn[t]
    vnb = [x.astype(BF16).astype(F32) for x in vn]
    for t in range(steps):
        z = sw_ref[steps, t:t + 1, :]
        for s in range(t + 1):
            z = z + sw_ref[s, t:t + 1, :] * vnb[s]
        yb_ref[slab(t), :] = (u_ref[slab(t), :] * z * gb_ref[slab(t), :].astype(F32)).astype(BF16)


def _mix_sample(auv, gates, ya, yb, state_pool, carried, pw, ps, sg, swx, layer, m_prompt, steps):
    m, width3 = auv.shape
    width = width3 // 3
    depth, _, nseq, _ = state_pool.shape
    rows = MIX_SEQS * steps
    first = m_prompt // rows
    row = lambda i: first + i
    any_spec = pl.BlockSpec(memory_space=pl.ANY)
    aliased = (ya, yb) + (tuple(carried) if carried is not None else ())
    blocked = [
        (auv, pl.BlockSpec((rows, width), lambda i: (row(i), 0))),
        (auv, pl.BlockSpec((rows, width), lambda i: (row(i), 1))),
        (auv, pl.BlockSpec((rows, width), lambda i: (row(i), 2))),
        (gates, pl.BlockSpec((rows, width), lambda i: (row(i), 0))),
        (gates, pl.BlockSpec((rows, width), lambda i: (row(i), 1))),
        (state_pool, pl.BlockSpec((None, POOL_BUF, MIX_SEQS, width), lambda i: (layer, 0, i, 0))),
        (pw, pl.BlockSpec((None,) + pw.shape[1:], lambda i: (layer, 0, 0, 0))),
        (ps, pl.BlockSpec((None, 1, width), lambda i: (layer, 0, 0))),
        (sg, pl.BlockSpec((None, 1, width), lambda i: (layer, 0, 0))),
        (swx, pl.BlockSpec((None,) + swx.shape[1:], lambda i: (layer, 0, 0, 0))),
    ]
    return pl.pallas_call(
        functools.partial(_mix_sample_kernel, len(aliased)),
        grid=(nseq // MIX_SEQS,),
        in_specs=[spec for _, spec in blocked] + [any_spec] * len(aliased),
        out_specs=[
            pl.BlockSpec((rows, width), lambda i: (row(i), 0)),
            pl.BlockSpec((rows, width), lambda i: (row(i), 0)),
            pl.BlockSpec((None, POOL_BUF, MIX_SEQS, width), lambda i: (layer, 0, i, 0)),
            pl.BlockSpec((None, MIX_SEQS, steps, width), lambda i: (layer, i, 0, 0)),
        ],
        out_shape=[jax.ShapeDtypeStruct(ya.shape, ya.dtype),
                   jax.ShapeDtypeStruct(yb.shape, yb.dtype),
                   jax.ShapeDtypeStruct((depth, POOL_BUF, nseq, width), F32),
                   jax.ShapeDtypeStruct((depth, nseq, steps, width), F32)],
        input_output_aliases={len(blocked) + k: k for k in range(len(aliased))},
        compiler_params=_params("arbitrary"),
        name="mix_sample",
    )(*[arr for arr, _ in blocked], *aliased)


def _merge_kernel(layer, ya_ref, yb_ref, wa_hbm, wb_hbm, sa_ref, sb_ref, o_ref, *scratch):
    k = ya_ref.shape[1]
    per = k // W_CHUNK

    def chunk_src(c, jt):
        if c < per:
            return _col_window(wa_hbm, layer, c, jt)
        return _col_window(wb_hbm, layer, c - per, jt)

    def body(w_ref):
        for q in range(COL_TILE // DOT_COLS):
            cs = slice(q * DOT_COLS, (q + 1) * DOT_COLS)
            pa = jnp.dot(ya_ref[...], w_ref[0:k, cs], preferred_element_type=F32)
            pb = jnp.dot(yb_ref[...], w_ref[k:2 * k, cs], preferred_element_type=F32)
            o_ref[:, cs] = (sa_ref[:, cs].astype(F32) * pa + sb_ref[:, cs].astype(F32) * pb).astype(o_ref.dtype)

    _weight_pipeline(chunk_src, 2 * per, *scratch, body)


def _merge(ya, yb, wa, wb, sel, layer):
    m, k = ya.shape
    n = wa.shape[-1]
    nb = n // COL_TILE
    assert m // ROW_TILE >= 2 * k // W_CHUNK + 1
    any_spec = pl.BlockSpec(memory_space=pl.ANY)
    return pl.pallas_call(
        functools.partial(_merge_kernel, layer),
        grid=(nb, m // ROW_TILE),
        in_specs=[pl.BlockSpec((ROW_TILE, k), lambda j, i: (i, 0)),
                  pl.BlockSpec((ROW_TILE, k), lambda j, i: (i, 0)),
                  any_spec, any_spec,
                  pl.BlockSpec((ROW_TILE, COL_TILE), lambda j, i: (i, j)),
                  pl.BlockSpec((ROW_TILE, COL_TILE), lambda j, i: (i, j + nb))],
        out_specs=pl.BlockSpec((ROW_TILE, COL_TILE), lambda j, i: (i, j)),
        out_shape=jax.ShapeDtypeStruct((m, n), BF16),
        scratch_shapes=_weight_scratch(2 * k),
        compiler_params=_params("arbitrary", "arbitrary"),
        name="branch_merge",
    )(ya, yb, wa, wb, sel, sel)


def _out_proj_kernel(layer, feeds_next, mg_ref, w_hbm, x_ref, *refs):
    if feeds_next:
        g_ref, o_ref, xb_ref, ssq_ref = refs[:4]
    else:
        o_ref = refs[0]
    k = mg_ref.shape[1]

    def body(w_ref):
        parts = []
        for q in range(COL_TILE // DOT_COLS):
            cs = slice(q * DOT_COLS, (q + 1) * DOT_COLS)
            y = x_ref[:, cs] + jnp.dot(mg_ref[...], w_ref[:, cs], preferred_element_type=F32)
            o_ref[:, cs] = y
            if feeds_next:
                xb_ref[:, cs] = (y * g_ref[:, cs]).astype(BF16)
                parts += _lane_partials(y * y, 1)
        if feeds_next:
            ssq_ref[...] = functools.reduce(lambda a, b: a + b, parts)

    _weight_pipeline(lambda c, jt: _col_window(w_hbm, layer, c, jt), k // W_CHUNK, *refs[-4:], body)


def _out_proj(merged, w_out, x, layer, g_all):
    m, k = merged.shape
    n = w_out.shape[-1]
    assert m // ROW_TILE >= k // W_CHUNK + 1
    feeds_next = g_all is not None
    tile = pl.BlockSpec((ROW_TILE, COL_TILE), lambda j, i: (i, j))
    in_specs = [pl.BlockSpec((ROW_TILE, k), lambda j, i: (i, 0)), pl.BlockSpec(memory_space=pl.ANY), tile]
    args = [merged, w_out, x]
    out_specs, out_shape = [tile], [jax.ShapeDtypeStruct((m, n), F32)]
    if feeds_next:
        assert n // COL_TILE == SSQ_GROUPS
        in_specs.append(pl.BlockSpec((None, 1, COL_TILE), lambda j, i: (layer + 1, 0, j)))
        args.append(g_all)
        out_specs += [tile, pl.BlockSpec((ROW_TILE, LANES), lambda j, i: (i, j))]
        out_shape += [jax.ShapeDtypeStruct((m, n), BF16), jax.ShapeDtypeStruct((m, SSQ_GROUPS * LANES), F32)]
    return pl.pallas_call(
        functools.partial(_out_proj_kernel, layer, feeds_next),
        grid=(n // COL_TILE, m // ROW_TILE),
        in_specs=in_specs,
        out_specs=out_specs,
        out_shape=out_shape,
        scratch_shapes=_weight_scratch(k),
        compiler_params=_params("arbitrary", "arbitrary"),
        name="out_proj",
    )(*args)


def kernel(x_prompt, x_sample, state_pool, norm_g, w_in, pool_w, pool_scale, sgu_norm_g, sgu_w, sgu_b,
           w_branch_a, w_branch_b, w_out, final_norm_g):
    batch, seq, d = x_prompt.shape
    nseq, steps, _ = x_sample.shape
    depth = w_in.shape[0]
    width = pool_scale.shape[-1]
    hdim = width // SGU_HEADS
    m_prompt = batch * seq
    assert PAST_LEN % CHUNK == 0 and steps <= CHUNK
    wt = width // COL_TILE
    ident_cols = lambda j: jnp.where(j < wt, j, j + wt)
    silu_cols = lambda j: jnp.where(j < wt, j + wt, j + 3 * wt)
    sigm_cols = lambda j: j + 5 * wt

    seq_tiles = nseq // MIX_SEQS
    xs = jnp.swapaxes(x_sample.reshape(seq_tiles, MIX_SEQS, steps, d), 1, 2).reshape(nseq * steps, d)
    g_all = norm_g.reshape(depth, 1, d)
    x, xb, ssq = _prep(x_prompt.reshape(m_prompt, d), xs, g_all[0])
    pw = pool_w.astype(BF16)
    ps = pool_scale.reshape(depth, 1, width)
    sg = sgu_norm_g.reshape(depth, 1, width)
    sbt = jnp.swapaxes(sgu_b, 1, 2)
    sw_small = sgu_w[:, :, :steps, :steps].astype(BF16).astype(F32)
    swx = jnp.repeat(jnp.transpose(sw_small, (0, 3, 2, 1)), hdim, axis=-1)
    bx = jnp.repeat(jnp.swapaxes(sgu_b[:, :, :steps], 1, 2), hdim, axis=-1)[:, None]
    swx = jnp.concatenate([swx, bx], axis=1)

    pool_t = jnp.swapaxes(state_pool, 1, 2)
    pool_p, carried = [], None
    for l in range(depth):
        auv, r = _proj(xb, ssq, w_in, l, ident_cols, 3 * wt, "none", F32, emits_r=True)
        gates, = _proj(xb, r, w_in, l, silu_cols, 2 * wt, "silu", BF16)
        sel, ya, yb, st_p = _proj(xb, r, w_in, l, sigm_cols, 2 * (d // COL_TILE), "sigmoid", BF16,
                                  mix_args=(auv, gates, pw, ps, sg, sgu_w, sbt, batch, seq))
        ya, yb, *carried = _mix_sample(auv, gates, ya, yb, pool_t, carried, pw, ps, sg, swx, l,
                                       m_prompt, steps)
        merged = _merge(ya, yb, w_branch_a, w_branch_b, sel, l)
        if l + 1 < depth:
            x, xb, ssq = _out_proj(merged, w_out, x, l, g_all)
        else:
            x, = _out_proj(merged, w_out, x, l, None)
        pool_p.append(st_p)

    y_p, y_s = _final_norm(x, final_norm_g.reshape(1, d), m_prompt)
    y_s = jnp.swapaxes(y_s.reshape(seq_tiles, steps, MIX_SEQS, d), 1, 2)
    return (y_p.reshape(batch, seq, d), y_s.reshape(nseq, steps, d),
            jnp.stack(pool_p, axis=0), jnp.swapaxes(carried[0], 1, 2), carried[1])
```

```python
import functools

import jax
import jax.numpy as jnp
from jax import lax
from jax.experimental import pallas as pl
from jax.experimental.pallas import tpu as pltpu

F32 = jnp.float32
BF16 = jnp.bfloat16

POOL_WINDOWS = (2, 4, 8, 16)
POOL_GROUPS = len(POOL_WINDOWS)
POOL_BUF = max(POOL_WINDOWS) - 1
HALO = POOL_BUF + 1
SGU_HEADS = 8
CHUNK = 128
EPS = 1e-6
PAST_LEN = 16384

V7X_VMEM_LIMIT_BYTES = 60 * 1024 * 1024

LANES = 128
SUBLANES = 8
ROW_TILE = 1024
COL_TILE = 1024
DOT_COLS = 256
W_CHUNK = 512
SSQ_GROUPS = 4
PREP_ROWS = 256
NORM_ROWS = 512
MIX_ROWS = 128
MIX_FRONT = 2 * HALO
MIX_SEQS = 32


def _params(*semantics):
    return pltpu.CompilerParams(dimension_semantics=semantics,
                                vmem_limit_bytes=V7X_VMEM_LIMIT_BYTES)


def _rms(x, g):
    ms = jnp.mean(x * x, axis=-1, keepdims=True)
    return x * lax.rsqrt(ms + EPS) * g


def _lane_partials(sq, groups):
    nblk = sq.shape[1] // LANES
    per = nblk // groups
    outs = []
    for q in range(groups):
        acc = sq[:, q * per * LANES:(q * per + 1) * LANES]
        for b in range(q * per + 1, (q + 1) * per):
            acc = acc + sq[:, b * LANES:(b + 1) * LANES]
        outs.append(acc)
    return outs


def _prep_kernel(n_prompt_tiles, xp_ref, xs_ref, g_ref, x_ref, xb_ref, ssq_ref):
    i = pl.program_id(0)

    def emit(src_ref):
        x = src_ref[...]
        x_ref[...] = x
        xb_ref[...] = (x * g_ref[...]).astype(BF16)
        for q, part in enumerate(_lane_partials(x * x, SSQ_GROUPS)):
            ssq_ref[:, q * LANES:(q + 1) * LANES] = part

    @pl.when(i < n_prompt_tiles)
    def _():
        emit(xp_ref)

    @pl.when(i >= n_prompt_tiles)
    def _():
        emit(xs_ref)


def _prep(xp, xs, g):
    mp, d = xp.shape
    m = mp + xs.shape[0]
    npt = mp // PREP_ROWS
    return pl.pallas_call(
        functools.partial(_prep_kernel, npt),
        grid=(m // PREP_ROWS,),
        in_specs=[pl.BlockSpec((PREP_ROWS, d), lambda i: (jnp.minimum(i, npt - 1), 0)),
                  pl.BlockSpec((PREP_ROWS, d), lambda i: (jnp.maximum(i - npt, 0), 0)),
                  pl.BlockSpec((1, d), lambda i: (0, 0))],
        out_specs=[pl.BlockSpec((PREP_ROWS, d), lambda i: (i, 0)),
                   pl.BlockSpec((PREP_ROWS, d), lambda i: (i, 0)),
                   pl.BlockSpec((PREP_ROWS, SSQ_GROUPS * LANES), lambda i: (i, 0))],
        out_shape=[jax.ShapeDtypeStruct((m, d), F32),
                   jax.ShapeDtypeStruct((m, d), BF16),
                   jax.ShapeDtypeStruct((m, SSQ_GROUPS * LANES), F32)],
        compiler_params=_params("arbitrary"),
        name="prep",
    )(xp, xs, g)


def _final_norm_kernel(n_prompt_tiles, x_ref, g_ref, op_ref, os_ref):
    y = _rms(x_ref[...], g_ref[...])
    i = pl.program_id(0)

    @pl.when(i < n_prompt_tiles)
    def _():
        op_ref[...] = y

    @pl.when(i >= n_prompt_tiles)
    def _():
        os_ref[...] = y


def _final_norm(x, g, m_prompt):
    m, d = x.shape
    npt = m_prompt // NORM_ROWS
    return pl.pallas_call(
        functools.partial(_final_norm_kernel, npt),
        grid=(m // NORM_ROWS,),
        in_specs=[pl.BlockSpec((NORM_ROWS, d), lambda i: (i, 0)),
                  pl.BlockSpec((1, d), lambda i: (0, 0))],
        out_specs=[pl.BlockSpec((NORM_ROWS, d), lambda i: (jnp.minimum(i, npt - 1), 0)),
                   pl.BlockSpec((NORM_ROWS, d), lambda i: (jnp.maximum(i - npt, 0), 0))],
        out_shape=[jax.ShapeDtypeStruct((m_prompt, d), F32),
                   jax.ShapeDtypeStruct((m - m_prompt, d), F32)],
        compiler_params=_params("arbitrary"),
        name="final_norm",
    )(x, g)


def _weight_scratch(k):
    return [pltpu.VMEM((k, COL_TILE), BF16),
            pltpu.VMEM((k, COL_TILE), BF16),
            pltpu.VMEM((2, W_CHUNK, COL_TILE), F32),
            pltpu.SemaphoreType.DMA((2,))]


def _weight_pipeline(chunk_src, n_chunks, w_even_ref, w_odd_ref, ring_ref, sem_ref, body):
    j, i = pl.program_id(0), pl.program_id(1)

    def copy(c, jt):
        return pltpu.make_async_copy(chunk_src(c, jt), ring_ref.at[c % 2], sem_ref.at[c % 2])

    @pl.when((j == 0) & (i == 0))
    def _():
        copy(0, 0).start()
        for c in range(n_chunks):
            if c + 1 < n_chunks:
                copy(c + 1, 0).start()
            copy(c, 0).wait()
            w_even_ref[c * W_CHUNK:(c + 1) * W_CHUNK, :] = ring_ref[c % 2].astype(BF16)

    has_next = j + 1 < pl.num_programs(0)
    for c in range(n_chunks):
        @pl.when(has_next & (i == c + 1))
        def _():
            copy(c, j + 1).wait()

        @pl.when(has_next & (i == c))
        def _():
            copy(c, j + 1).start(priority=1)

    def step(w_ref, next_ref):
        c = jnp.clip(i - 1, 0, n_chunks - 1)
        row0 = pl.multiple_of(c * W_CHUNK, W_CHUNK)
        body(w_ref)
        next_ref[pl.ds(row0, W_CHUNK), :] = ring_ref[(i + 1) % 2].astype(BF16)

    @pl.when(j % 2 == 0)
    def _():
        step(w_even_ref, w_odd_ref)

    @pl.when(j % 2 == 1)
    def _():
        step(w_odd_ref, w_even_ref)


def _col_window(w_hbm, layer, c, col_block):
    col0 = col_block * COL_TILE
    if not isinstance(col0, int):
        col0 = pl.multiple_of(col0, COL_TILE)
    return w_hbm.at[layer, pl.ds(c * W_CHUNK, W_CHUNK), pl.ds(col0, COL_TILE)]


N_MIX_IN = 11


def _proj_kernel(act, layer, col_block_of, emits_r, mix, xb_ref, stat_ref, w_hbm, *refs):
    refs = list(refs)
    mix_in = [refs.pop(0) for _ in range(N_MIX_IN)] if mix else []
    o_ref = refs.pop(0)
    r_ref = refs.pop(0) if emits_r else None
    mix_out = [refs.pop(0) for _ in range(3)] if mix else []
    scratch = refs
    k = xb_ref.shape[1]
    step = pl.program_id(0) * pl.num_programs(1) + pl.program_id(1)

    def row_scale():
        r = lax.rsqrt(jnp.sum(stat_ref[...], axis=-1, keepdims=True) / k + EPS)
        return jnp.broadcast_to(r, (r.shape[0], LANES))

    row_step = pl.program_id(1)
    if emits_r:
        rs_ref = scratch[4]

        @pl.when(pl.program_id(0) == 0)
        def _():
            r_lanes = row_scale()
            r_ref[...] = r_lanes
            rs_ref[row_step] = r_lanes

    if mix:
        tiles_per_seq, n_tiles = mix

        @pl.when(step < n_tiles)
        def _():
            _prompt_mix(step % tiles_per_seq, tiles_per_seq, *mix_in, *mix_out, *scratch[4:])

    def body(w_ref):
        r_lanes = rs_ref[row_step] if emits_r else stat_ref[...]
        r_cols = jnp.concatenate([r_lanes] * (DOT_COLS // LANES), axis=1)
        for q in range(COL_TILE // DOT_COLS):
            cs = slice(q * DOT_COLS, (q + 1) * DOT_COLS)
            p = jnp.dot(xb_ref[...], w_ref[:, cs], preferred_element_type=F32) * r_cols
            if act != "none":
                gate = 0.5 * jnp.tanh(0.5 * p) + 0.5
                p = p * gate if act == "silu" else gate
            o_ref[:, cs] = p.astype(o_ref.dtype)

    _weight_pipeline(lambda c, jt: _col_window(w_hbm, layer, c, col_block_of(jt)),
                     k // W_CHUNK, *scratch[:4], body)


def _proj(xb, stat, w_in, layer, col_block_of, n_col_blocks, act, out_dtype, emits_r=False, mix_args=None):
    m, k = xb.shape
    n_rows = m // ROW_TILE
    assert n_rows >= k // W_CHUNK + 1
    assert not (emits_r and mix_args is not None)
    tile = pl.BlockSpec((ROW_TILE, COL_TILE), lambda j, i: (i, j))
    in_specs = [pl.BlockSpec((ROW_TILE, k), lambda j, i: (i, 0)),
                pl.BlockSpec((ROW_TILE, stat.shape[1]), lambda j, i: (i, 0)),
                pl.BlockSpec(memory_space=pl.ANY)]
    args = [xb, stat, w_in]
    out_specs, out_shape = [tile], [jax.ShapeDtypeStruct((m, n_col_blocks * COL_TILE), out_dtype)]
    scratch = _weight_scratch(k)
    if emits_r:
        out_specs.append(pl.BlockSpec((ROW_TILE, LANES), lambda j, i: (jnp.where(j == 0, i, n_rows - 1), 0)))
        out_shape.append(jax.ShapeDtypeStruct((m, LANES), F32))
        scratch = scratch + [pltpu.VMEM((n_rows, ROW_TILE, LANES), F32)]
    mix = None
    if mix_args is not None:
        auv, gates, pw, ps, sg, sw, sbt, batch, seq = mix_args
        tiles_per_seq = seq // MIX_ROWS
        mix = (tiles_per_seq, batch * tiles_per_seq)
        assert n_col_blocks * n_rows >= mix[1]
        tile_of = lambda j, i: jnp.minimum(j * n_rows + i, mix[1] - 1)
        mi, margs, mo, mshape, mscratch = _prompt_mix_specs(auv, gates, pw, ps, sg, sw, sbt, layer, batch,
                                                             tiles_per_seq, tile_of)
        in_specs += mi
        args += margs
        out_specs += mo
        out_shape += mshape
        scratch = scratch + mscratch
    return pl.pallas_call(
        functools.partial(_proj_kernel, act, layer, col_block_of, emits_r, mix),
        grid=(n_col_blocks, n_rows),
        in_specs=in_specs,
        out_specs=out_specs,
        out_shape=out_shape,
        scratch_shapes=scratch,
        compiler_params=_params("arbitrary", "arbitrary"),
        name="in_proj_" + act,
    )(*args)


def _tril_bf16(w):
    n = w.shape[-1]
    row = lax.broadcasted_iota(jnp.int32, (n, n), 0)
    col = lax.broadcasted_iota(jnp.int32, (n, n), 1)
    return jnp.where(col <= row, w, 0.0).astype(BF16)


def _prompt_mix(s, tiles_per_seq, a_ref, halo_ref, u_ref, v_ref, ga_ref, gb_ref, pw_ref, ps_ref, sg_ref,
                sw_ref, sbt_ref, ya_ref, yb_ref, st_ref, f_ref, p_ref, q_ref):
    rows, width = a_ref.shape
    gdim = width // POOL_GROUPS
    hdim = width // SGU_HEADS
    end = MIX_FRONT + rows

    f_ref[0:MIX_FRONT - HALO, :] = jnp.zeros((MIX_FRONT - HALO, width), F32)
    f_ref[MIX_FRONT - HALO:MIX_FRONT, :] = jnp.where(s == 0, 0.0, halo_ref[...])
    f_ref[MIX_FRONT:, :] = a_ref[...]
    pos = s * rows + lax.broadcasted_iota(jnp.int32, (rows, 1), 0)
    for g, w in enumerate(POOL_WINDOWS):
        cs = slice(g * gdim, (g + 1) * gdim)
        tok = a_ref[:, cs]
        stages = w.bit_length() - 1
        assert 1 << stages == w and w // 2 <= SUBLANES and SUBLANES * (stages - 1) < MIX_FRONT
        src, cols = f_ref, cs
        for j in range(1, stages + 1):
            shift = 1 << (j - 1)
            start = MIX_FRONT - SUBLANES * (stages - j)
            acc = src[start:end, cols] + src[start - shift:end - shift, cols]
            if j < stages:
                dst = (p_ref, q_ref)[j % 2]
                dst[start:end, :] = acc
                src, cols = dst, slice(0, gdim)
        cnt = jnp.minimum(pos + 1, w).astype(F32)
        pooled = (acc / cnt - tok).astype(BF16)
        mixed = jnp.dot(pooled, pw_ref[g], preferred_element_type=F32)
        ya_ref[:, cs] = (mixed * ps_ref[:, cs] * ga_ref[:, cs].astype(F32)).astype(BF16)

    vn = _rms(v_ref[...], sg_ref[...]).astype(BF16)
    for h in range(SGU_HEADS):
        cs = slice(h * hdim, (h + 1) * hdim)
        wsh = _tril_bf16(sw_ref[h])
        bias = sbt_ref[:, h:h + 1]
        for c in range(rows // CHUNK):
            rs = slice(c * CHUNK, (c + 1) * CHUNK)
            z = jnp.dot(wsh, vn[rs, cs], preferred_element_type=F32) + bias
            yb_ref[rs, cs] = (u_ref[rs, cs] * z * gb_ref[rs, cs].astype(F32)).astype(BF16)

    @pl.when(s == tiles_per_seq - 1)
    def _():
        st_ref[...] = f_ref[end - POOL_BUF:end, :]


def _prompt_mix_specs(auv, gates, pw, ps, sg, sw, sbt, layer, batch, tiles_per_seq, tile_of):
    m, width3 = auv.shape
    width = width3 // 3
    halo_per_tile = MIX_ROWS // HALO
    once = pl.Buffered(1)
    rows = lambda col: pl.BlockSpec((MIX_ROWS, width), lambda *g: (tile_of(*g), col))
    in_specs = [
        rows(0),
        pl.BlockSpec((HALO, width), lambda *g: (jnp.maximum(tile_of(*g) * halo_per_tile - 1, 0), 0)),
        rows(1), rows(2), rows(0), rows(1),
        pl.BlockSpec((None,) + pw.shape[1:], lambda *g: (layer, 0, 0, 0), pipeline_mode=once),
        pl.BlockSpec((None, 1, width), lambda *g: (layer, 0, 0)),
        pl.BlockSpec((None, 1, width), lambda *g: (layer, 0, 0)),
        pl.BlockSpec((None,) + sw.shape[1:], lambda *g: (layer, 0, 0, 0), pipeline_mode=once),
        pl.BlockSpec((None,) + sbt.shape[1:], lambda *g: (layer, 0, 0)),
    ]
    assert len(in_specs) == N_MIX_IN
    args = [auv, auv, auv, auv, gates, gates, pw, ps, sg, sw, sbt]
    out_specs = [rows(0), rows(0),
                 pl.BlockSpec((None, POOL_BUF, width), lambda *g: (tile_of(*g) // tiles_per_seq, 0, 0))]
    out_shape = [jax.ShapeDtypeStruct((m, width), BF16),
                 jax.ShapeDtypeStruct((m, width), BF16),
                 jax.ShapeDtypeStruct((batch, POOL_BUF, width), F32)]
    scratch = [pltpu.VMEM((MIX_FRONT + MIX_ROWS, width), F32),
               pltpu.VMEM((MIX_FRONT + MIX_ROWS, width // POOL_GROUPS), F32),
               pltpu.VMEM((MIX_FRONT + MIX_ROWS, width // POOL_GROUPS), F32)]
    return in_specs, args, out_specs, out_shape, scratch


def _mix_sample_kernel(n_aliased, a_ref, u_ref, v_ref, ga_ref, gb_ref, buf_ref, pw_ref, ps_ref, sg_ref,
                       sw_ref, *refs):
    ya_ref, yb_ref, nb_ref, vn_ref = refs[n_aliased:]
    rows, width = a_ref.shape
    nseq = nb_ref.shape[1]
    steps = rows // nseq
    gdim = width // POOL_GROUPS
    slab = lambda t: slice(t * nseq, (t + 1) * nseq)

    def full(i, cs):
        if i < POOL_BUF:
            return buf_ref[i, :, cs]
        return a_ref[slab(i - POOL_BUF), cs]

    for r in range(POOL_BUF):
        nb_ref[r] = full(steps + r, slice(0, width))

    for t in range(steps):
        for g, w in enumerate(POOL_WINDOWS):
            cs = slice(g * gdim, (g + 1) * gdim)
            tok = full(POOL_BUF + t, cs)
            acc = tok
            for k in range(1, w):
                acc = acc + full(POOL_BUF + t - k, cs)
            cnt = float(min(PAST_LEN + t + 1, w))
            pooled = (acc / cnt - tok).astype(BF16)
            mixed = jnp.dot(pooled, pw_ref[g], preferred_element_type=F32)
            ya_ref[slab(t), cs] = (mixed * ps_ref[:, cs] * ga_ref[slab(t), cs].astype(F32)).astype(BF16)

    vn = [_rms(v_ref[slab(t), :], sg_ref[...]) for t in range(steps)]
    for t in range(steps):
        vn_ref[:, t, :] = vn[t]
    vnb = [x.astype(BF16).astype(F32) for x in vn]
    for t in range(steps):
        z = sw_ref[steps, t:t + 1, :]
        for s in range(t + 1):
            z = z + sw_ref[s, t:t + 1, :] * vnb[s]
        yb_ref[slab(t), :] = (u_ref[slab(t), :] * z * gb_ref[slab(t), :].astype(F32)).astype(BF16)


def _mix_sample(auv, gates, ya, yb, state_pool, carried, pw, ps, sg, swx, layer, m_prompt, steps):
    m, width3 = auv.shape
    width = width3 // 3
    depth, _, nseq, _ = state_pool.shape
    rows = MIX_SEQS * steps
    first = m_prompt // rows
    row = lambda i: first + i
    any_spec = pl.BlockSpec(memory_space=pl.ANY)
    aliased = (ya, yb) + (tuple(carried) if carried is not None else ())
    blocked = [
        (auv, pl.BlockSpec((rows, width), lambda i: (row(i), 0))),
        (auv, pl.BlockSpec((rows, width), lambda i: (row(i), 1))),
        (auv, pl.BlockSpec((rows, width), lambda i: (row(i), 2))),
        (gates, pl.BlockSpec((rows, width), lambda i: (row(i), 0))),
        (gates, pl.BlockSpec((rows, width), lambda i: (row(i), 1))),
        (state_pool, pl.BlockSpec((None, POOL_BUF, MIX_SEQS, width), lambda i: (layer, 0, i, 0))),
        (pw, pl.BlockSpec((None,) + pw.shape[1:], lambda i: (layer, 0, 0, 0))),
        (ps, pl.BlockSpec((None, 1, width), lambda i: (layer, 0, 0))),
        (sg, pl.BlockSpec((None, 1, width), lambda i: (layer, 0, 0))),
        (swx, pl.BlockSpec((None,) + swx.shape[1:], lambda i: (layer, 0, 0, 0))),
    ]
    return pl.pallas_call(
        functools.partial(_mix_sample_kernel, len(aliased)),
        grid=(nseq // MIX_SEQS,),
        in_specs=[spec for _, spec in blocked] + [any_spec] * len(aliased),
        out_specs=[
            pl.BlockSpec((rows, width), lambda i: (row(i), 0)),
            pl.BlockSpec((rows, width), lambda i: (row(i), 0)),
            pl.BlockSpec((None, POOL_BUF, MIX_SEQS, width), lambda i: (layer, 0, i, 0)),
            pl.BlockSpec((None, MIX_SEQS, steps, width), lambda i: (layer, i, 0, 0)),
        ],
        out_shape=[jax.ShapeDtypeStruct(ya.shape, ya.dtype),
                   jax.ShapeDtypeStruct(yb.shape, yb.dtype),
                   jax.ShapeDtypeStruct((depth, POOL_BUF, nseq, width), F32),
                   jax.ShapeDtypeStruct((depth, nseq, steps, width), F32)],
        input_output_aliases={len(blocked) + k: k for k in range(len(aliased))},
        compiler_params=_params("arbitrary"),
        name="mix_sample",
    )(*[arr for arr, _ in blocked], *aliased)


def _merge_kernel(layer, ya_ref, yb_ref, wa_hbm, wb_hbm, sa_ref, sb_ref, o_ref, *scratch):
    k = ya_ref.shape[1]
    per = k // W_CHUNK

    def chunk_src(c, jt):
        if c < per:
            return _col_window(wa_hbm, layer, c, jt)
        return _col_window(wb_hbm, layer, c - per, jt)

    def body(w_ref):
        for q in range(COL_TILE // DOT_COLS):
            cs = slice(q * DOT_COLS, (q + 1) * DOT_COLS)
            pa = jnp.dot(ya_ref[...], w_ref[0:k, cs], preferred_element_type=F32)
            pb = jnp.dot(yb_ref[...], w_ref[k:2 * k, cs], preferred_element_type=F32)
            o_ref[:, cs] = (sa_ref[:, cs].astype(F32) * pa + sb_ref[:, cs].astype(F32) * pb).astype(o_ref.dtype)

    _weight_pipeline(chunk_src, 2 * per, *scratch, body)


def _merge(ya, yb, wa, wb, sel, layer):
    m, k = ya.shape
    n = wa.shape[-1]
    nb = n // COL_TILE
    assert m // ROW_TILE >= 2 * k // W_CHUNK + 1
    any_spec = pl.BlockSpec(memory_space=pl.ANY)
    return pl.pallas_call(
        functools.partial(_merge_kernel, layer),
        grid=(nb, m // ROW_TILE),
        in_specs=[pl.BlockSpec((ROW_TILE, k), lambda j, i: (i, 0)),
                  pl.BlockSpec((ROW_TILE, k), lambda j, i: (i, 0)),
                  any_spec, any_spec,
                  pl.BlockSpec((ROW_TILE, COL_TILE), lambda j, i: (i, j)),
                  pl.BlockSpec((ROW_TILE, COL_TILE), lambda j, i: (i, j + nb))],
        out_specs=pl.BlockSpec((ROW_TILE, COL_TILE), lambda j, i: (i, j)),
        out_shape=jax.ShapeDtypeStruct((m, n), BF16),
        scratch_shapes=_weight_scratch(2 * k),
        compiler_params=_params("arbitrary", "arbitrary"),
        name="branch_merge",
    )(ya, yb, wa, wb, sel, sel)


def _out_proj_kernel(layer, feeds_next, mg_ref, w_hbm, x_ref, *refs):
    if feeds_next:
        g_ref, o_ref, xb_ref, ssq_ref = refs[:4]
    else:
        o_ref = refs[0]
    k = mg_ref.shape[1]

    def body(w_ref):
        parts = []
        for q in range(COL_TILE // DOT_COLS):
            cs = slice(q * DOT_COLS, (q + 1) * DOT_COLS)
            y = x_ref[:, cs] + jnp.dot(mg_ref[...], w_ref[:, cs], preferred_element_type=F32)
            o_ref[:, cs] = y
            if feeds_next:
                xb_ref[:, cs] = (y * g_ref[:, cs]).astype(BF16)
                parts += _lane_partials(y * y, 1)
        if feeds_next:
            ssq_ref[...] = functools.reduce(lambda a, b: a + b, parts)

    _weight_pipeline(lambda c, jt: _col_window(w_hbm, layer, c, jt), k // W_CHUNK, *refs[-4:], body)


def _out_proj(merged, w_out, x, layer, g_all):
    m, k = merged.shape
    n = w_out.shape[-1]
    assert m // ROW_TILE >= k // W_CHUNK + 1
    feeds_next = g_all is not None
    tile = pl.BlockSpec((ROW_TILE, COL_TILE), lambda j, i: (i, j))
    in_specs = [pl.BlockSpec((ROW_TILE, k), lambda j, i: (i, 0)), pl.BlockSpec(memory_space=pl.ANY), tile]
    args = [merged, w_out, x]
    out_specs, out_shape = [tile], [jax.ShapeDtypeStruct((m, n), F32)]
    if feeds_next:
        assert n // COL_TILE == SSQ_GROUPS
        in_specs.append(pl.BlockSpec((None, 1, COL_TILE), lambda j, i: (layer + 1, 0, j)))
        args.append(g_all)
        out_specs += [tile, pl.BlockSpec((ROW_TILE, LANES), lambda j, i: (i, j))]
        out_shape += [jax.ShapeDtypeStruct((m, n), BF16), jax.ShapeDtypeStruct((m, SSQ_GROUPS * LANES), F32)]
    return pl.pallas_call(
        functools.partial(_out_proj_kernel, layer, feeds_next),
        grid=(n // COL_TILE, m // ROW_TILE),
        in_specs=in_specs,
        out_specs=out_specs,
        out_shape=out_shape,
        scratch_shapes=_weight_scratch(k),
        compiler_params=_params("arbitrary", "arbitrary"),
        name="out_proj",
    )(*args)


def kernel(x_prompt, x_sample, state_pool, norm_g, w_in, pool_w, pool_scale, sgu_norm_g, sgu_w, sgu_b,
           w_branch_a, w_branch_b, w_out, final_norm_g):
    batch, seq, d = x_prompt.shape
    nseq, steps, _ = x_sample.shape
    depth = w_in.shape[0]
    width = pool_scale.shape[-1]
    hdim = width // SGU_HEADS
    m_prompt = batch * seq
    assert PAST_LEN % CHUNK == 0 and steps <= CHUNK
    wt = width // COL_TILE
    ident_cols = lambda j: jnp.where(j < wt, j, j + wt)
    silu_cols = lambda j: jnp.where(j < wt, j + wt, j + 3 * wt)
    sigm_cols = lambda j: j + 5 * wt

    seq_tiles = nseq // MIX_SEQS
    xs = jnp.swapaxes(x_sample.reshape(seq_tiles, MIX_SEQS, steps, d), 1, 2).reshape(nseq * steps, d)
    g_all = norm_g.reshape(depth, 1, d)
    x, xb, ssq = _prep(x_prompt.reshape(m_prompt, d), xs, g_all[0])
    pw = pool_w.astype(BF16)
    ps = pool_scale.reshape(depth, 1, width)
    sg = sgu_norm_g.reshape(depth, 1, width)
    sbt = jnp.swapaxes(sgu_b, 1, 2)
    sw_small = sgu_w[:, :, :steps, :steps].astype(BF16).astype(F32)
    swx = jnp.repeat(jnp.transpose(sw_small, (0, 3, 2, 1)), hdim, axis=-1)
    bx = jnp.repeat(jnp.swapaxes(sgu_b[:, :, :steps], 1, 2), hdim, axis=-1)[:, None]
    swx = jnp.concatenate([swx, bx], axis=1)

    pool_t = jnp.swapaxes(state_pool, 1, 2)
    pool_p, carried = [], None
    for l in range(depth):
        auv, r = _proj(xb, ssq, w_in, l, ident_cols, 3 * wt, "none", F32, emits_r=True)
        gates, = _proj(xb, r, w_in, l, silu_cols, 2 * wt, "silu", BF16)
        sel, ya, yb, st_p = _proj(xb, r, w_in, l, sigm_cols, 2 * (d // COL_TILE), "sigmoid", BF16,
                                  mix_args=(auv, gates, pw, ps, sg, sgu_w, sbt, batch, seq))
        ya, yb, *carried = _mix_sample(auv, gates, ya, yb, pool_t, carried, pw, ps, sg, swx, l,
                                       m_prompt, steps)
        merged = _merge(ya, yb, w_branch_a, w_branch_b, sel, l)
        if l + 1 < depth:
            x, xb, ssq = _out_proj(merged, w_out, x, l, g_all)
        else:
            x, = _out_proj(merged, w_out, x, l, None)
        pool_p.append(st_p)

    y_p, y_s = _final_norm(x, final_norm_g.reshape(1, d), m_prompt)
    y_s = jnp.swapaxes(y_s.reshape(seq_tiles, steps, MIX_SEQS, d), 1, 2)
    return (y_p.reshape(batch, seq, d), y_s.reshape(nseq, steps, d),
            jnp.stack(pool_p, axis=0), jnp.swapaxes(carried[0], 1, 2), carried[1])
```
